```python
import jax, jax.numpy as jnp
from jax import lax
import numpy as np

D_MODEL = 1024
BATCH = 4
SEQ = 4096
DEPTH = 1
DEC_BATCH = 32
DEC_SEQ = 16
PAST_LEN = 1024

CHUNK = 64
Q_BLOCK = 2 * CHUNK
H_A = 16
D_HEAD_A = 64
D_A = H_A * D_HEAD_A
H_B = 16
D_HEAD_B = 64
D_B = H_B * D_HEAD_B
W_LORA = 64
A_LORA = 64
C_A = 4 * D_A
C_GATE = 2 * D_MODEL
C_SHIFT = 3 * D_B + W_LORA + A_LORA + D_B
N_IN = C_A + C_GATE + C_SHIFT
B_SPLITS = (D_B, 2 * D_B, 3 * D_B, 3 * D_B + W_LORA, 3 * D_B + W_LORA + A_LORA)
EPS = 1e-6
LNX_EPS = 64e-5

kernel_name = "hybrid_stickbreak_rwkv7_stream_step"


def _rmsnorm(x, g):
    xf = x.astype(jnp.float32)
    y = xf * lax.rsqrt(jnp.mean(xf * xf, axis=-1, keepdims=True) + EPS)
    return (y * g.astype(jnp.float32)).astype(x.dtype)


def _sb_attend(q, k, v, q_pos, k_pos):
    z = jnp.einsum('bhqd,bhkd->bhqk', q.astype(jnp.float32), k.astype(jnp.float32)) * (D_HEAD_A ** -0.5)
    valid = k_pos[None, :] < q_pos[:, None]
    log_keep = jnp.where(valid, jax.nn.log_sigmoid(-z), 0.0)
    later = lax.cumsum(log_keep, axis=3, reverse=True) - log_keep
    weight = jnp.where(valid, jnp.exp(jax.nn.log_sigmoid(z) + later), 0.0)
    o = jnp.einsum('bhqk,bhkd->bhqd', weight, v.astype(jnp.float32))
    return o.astype(q.dtype)


def _sb_prompt(q, k, v):
    T = q.shape[2]
    outs = []
    for i in range(T // Q_BLOCK):
        start, end = i * Q_BLOCK, (i + 1) * Q_BLOCK
        outs.append(_sb_attend(q[:, :, start:end], k[:, :, :end], v[:, :, :end],
                               jnp.arange(start, end), jnp.arange(end)))
    return jnp.concatenate(outs, axis=2)


def _wkv_scan(r, w, k, v, kk, a, S0):
    def step(S, inp):
        r_t, w_t, k_t, v_t, kk_t, a_t = inp
        sa = jnp.einsum('bhvk,bhk->bhv', S, -kk_t)
        S = (S * w_t[:, :, None, :] + sa[..., None] * (kk_t * a_t)[:, :, None, :]
             + v_t[..., None] * k_t[:, :, None, :])
        return S, jnp.einsum('bhvk,bhk->bhv', S, r_t)
    xs = (r.swapaxes(0, 1), w.swapaxes(0, 1), k.swapaxes(0, 1), v.swapaxes(0, 1),
          kk.swapaxes(0, 1), a.swapaxes(0, 1))
    S, o = lax.scan(step, S0, xs)
    return S, o.swapaxes(0, 1)


def _layer(x, shift_prev, k_past, v_past, wkv_prev, norm_g, w_in, mu_shift, w0, w2, a0, a2,
           k_k, k_a, r_k, lnx_g, lnx_b, w_o_a, w_o_b, w_out):
    B, T, _ = x.shape
    f32 = jnp.float32
    h = _rmsnorm(x, norm_g)
    proj = jnp.einsum('btd,dc->btc', h, w_in)
    p_a = proj[..., :C_A]
    p_gate = proj[..., C_A:C_A + C_GATE]
    p_b = proj[..., C_A + C_GATE:]

    q, k, v, z_a = jnp.split(p_a, 4, axis=-1)
    q = q.reshape(B, T, H_A, D_HEAD_A).transpose(0, 2, 1, 3)
    k = k.reshape(B, T, H_A, D_HEAD_A).transpose(0, 2, 1, 3)
    v = v.reshape(B, T, H_A, D_HEAD_A).transpose(0, 2, 1, 3)
    if k_past is None:
        o_a = _sb_prompt(q, k, v)
    else:
        past = k_past.shape[2]
        k_all = jnp.concatenate([k_past.astype(k.dtype), k], axis=2)
        v_all = jnp.concatenate([v_past.astype(v.dtype), v], axis=2)
        o_a = _sb_attend(q, k_all, v_all, past + jnp.arange(T), jnp.arange(past + T))
    o_a = o_a.transpose(0, 2, 1, 3).reshape(B, T, D_A)
    y_a = jnp.einsum('btc,cd->btd', o_a * jax.nn.silu(z_a), w_o_a)

    p_prev = jnp.concatenate([shift_prev.astype(p_b.dtype), p_b[:, :-1]], axis=1)
    p_mix = p_b + mu_shift * (p_prev - p_b)
    shift_new = p_b[:, -1:]
    r, kb, vb, lat_w, lat_a, z_b = jnp.split(p_mix, B_SPLITS, axis=-1)
    w_raw = -jax.nn.softplus(-(w0 + jnp.einsum('btr,rc->btc', jnp.tanh(lat_w), w2))) - 0.5
    decay = jnp.exp(-jnp.exp(w_raw.astype(f32)))
    a = jax.nn.sigmoid(a0 + jnp.einsum('btr,rc->btc', lat_a, a2))
    kk = (kb * k_k).reshape(B, T, H_B, D_HEAD_B).astype(f32)
    kk = kk / jnp.maximum(jnp.sqrt(jnp.sum(kk * kk, axis=-1, keepdims=True)), 1e-12)
    kb = kb * (1.0 + (a - 1.0) * k_a)
    r_h = r.reshape(B, T, H_B, D_HEAD_B).astype(f32)
    k_h = kb.reshape(B, T, H_B, D_HEAD_B).astype(f32)
    v_h = vb.reshape(B, T, H_B, D_HEAD_B).astype(f32)
    a_h = a.reshape(B, T, H_B, D_HEAD_B).astype(f32)
    w_h = decay.reshape(B, T, H_B, D_HEAD_B)
    wkv_new, o_b = _wkv_scan(r_h, w_h, k_h, v_h, kk, a_h, wkv_prev.astype(f32))
    mean = jnp.mean(o_b, axis=-1, keepdims=True)
    var = jnp.mean(jnp.square(o_b - mean), axis=-1, keepdims=True)
    o_b = ((o_b - mean) * lax.rsqrt(var + LNX_EPS)).reshape(B, T, D_B)
    o_b = o_b * lnx_g.astype(f32) + lnx_b.astype(f32)
    bonus = jnp.sum(r_h * k_h * r_k.astype(f32), axis=-1, keepdims=True) * v_h
    o_b = (o_b + bonus.reshape(B, T, D_B)).astype(x.dtype)
    y_b = jnp.einsum('btc,cd->btd', o_b * jax.nn.silu(z_b), w_o_b)

    g_a, g_b = jnp.split(jax.nn.sigmoid(p_gate), 2, axis=-1)
    y = jnp.einsum('btd,de->bte', g_a * y_a + g_b * y_b, w_out)
    return x + y, k, v, shift_new, wkv_new


def setup_inputs(seed: int = 0) -> dict:
    key = jax.random.key(seed)
    ks = jax.random.split(key, 24)
    nrm = lambda k, shape, s: jax.random.normal(k, shape, jnp.float32) * s
    L = DEPTH
    return {
        "x_prompt": nrm(ks[0], (BATCH, SEQ, D_MODEL), 1.0),
        "x_sample": nrm(ks[1], (DEC_BATCH, DEC_SEQ, D_MODEL), 1.0),
        "cache_sb_k": nrm(ks[2], (L, DEC_BATCH, H_A, PAST_LEN, D_HEAD_A), 1.0),
        "cache_sb_v": nrm(ks[3], (L, DEC_BATCH, H_A, PAST_LEN, D_HEAD_A), 1.0),
        "state_shift": nrm(ks[4], (L, DEC_BATCH, 1, C_SHIFT), 1.0),
        "state_wkv": nrm(ks[5], (L, DEC_BATCH, H_B, D_HEAD_B, D_HEAD_B), 0.5),
        "norm_g": 1.0 + nrm(ks[6], (L, D_MODEL), 0.02),
        "w_in": nrm(ks[7], (L, D_MODEL, N_IN), D_MODEL ** -0.5),
        "mu_shift": jax.random.uniform(ks[8], (L, C_SHIFT), jnp.float32),
        "w0": jax.random.uniform(ks[9], (L, D_B), jnp.float32, -4.0, 0.0),
        "w2": nrm(ks[10], (L, W_LORA, D_B), 0.1 * W_LORA ** -0.5),
        "a0": nrm(ks[11], (L, D_B), 0.1),
        "a2": nrm(ks[12], (L, A_LORA, D_B), 0.1 * A_LORA ** -0.5),
        "k_k": 0.85 + nrm(ks[13], (L, D_B), 0.02),
        "k_a": 1.0 + nrm(ks[14], (L, D_B), 0.02),
        "r_k": nrm(ks[15], (L, H_B, D_HEAD_B), 0.1),
        "lnx_g": 1.0 + nrm(ks[16], (L, D_B), 0.02),
        "lnx_b": nrm(ks[17], (L, D_B), 0.02),
        "w_o_a": nrm(ks[18], (L, D_A, D_MODEL), D_A ** -0.5),
        "w_o_b": nrm(ks[19], (L, D_B, D_MODEL), D_B ** -0.5),
        "w_out": nrm(ks[20], (L, D_MODEL, D_MODEL), D_MODEL ** -0.5),
        "final_norm_g": 1.0 + nrm(ks[21], (D_MODEL,), 0.02),
    }


def reference(x_prompt, x_sample, cache_sb_k, cache_sb_v, state_shift, state_wkv,
              norm_g, w_in, mu_shift, w0, w2, a0, a2, k_k, k_a, r_k, lnx_g, lnx_b,
              w_o_a, w_o_b, w_out, final_norm_g):
    xp, xs = x_prompt, x_sample
    B = xp.shape[0]
    kp_l, vp_l, sp_l, wp_l = [], [], [], []
    ks_l, vs_l, ss_l, ws_l = [], [], [], []
    for l in range(DEPTH):
        params = (norm_g[l], w_in[l], mu_shift[l], w0[l], w2[l], a0[l], a2[l], k_k[l], k_a[l],
                  r_k[l], lnx_g[l], lnx_b[l], w_o_a[l], w_o_b[l], w_out[l])
        zero_shift = jnp.zeros((B, 1, C_SHIFT), xp.dtype)
        zero_wkv = jnp.zeros((B, H_B, D_HEAD_B, D_HEAD_B), jnp.float32)
        xp, kp, vp, sp, wp = _layer(xp, zero_shift, None, None, zero_wkv, *params)
        xs, kss, vss, sss, wss = _layer(xs, state_shift[l], cache_sb_k[l], cache_sb_v[l],
                                        state_wkv[l], *params)
        kp_l.append(kp); vp_l.append(vp); sp_l.append(sp); wp_l.append(wp)
        ks_l.append(kss); vs_l.append(vss); ss_l.append(sss); ws_l.append(wss)
    y_prompt = _rmsnorm(xp, final_norm_g)
    y_sample = _rmsnorm(xs, final_norm_g)
    return (y_prompt, y_sample,
            jnp.stack(kp_l), jnp.stack(vp_l), jnp.stack(sp_l), jnp.stack(wp_l),
            jnp.stack(ks_l), jnp.stack(vs_l), jnp.stack(ss_l), jnp.stack(ws_l))
```

```python
import functools

import jax
import jax.numpy as jnp
from jax import lax
from jax.experimental import pallas as pl
from jax.experimental.pallas import tpu as pltpu

F32 = jnp.float32
BF16 = jnp.bfloat16

D_MODEL = 1024
N_HEADS = 16
D_HEAD = 64
LANES = 128
SUBLANES = 8
N_PAIRS = N_HEADS // 2
C_QKVG = 6 * D_MODEL
C_SHIFT = 3 * D_MODEL + 2 * D_HEAD + D_MODEL
LORA_OFF = 3 * D_MODEL
ZB_OFF = LORA_OFF + LANES
EPS = 1e-6
LNX_EPS = 64e-5
KEY_BLOCK = 128
RWKV_GROUP = 2
RWKV_CHUNK = 64
VMEM_LIMIT = 56 * 1024 * 1024


def _softplus(x):
    return jnp.maximum(x, 0.0) + jnp.log1p(jnp.exp(-jnp.abs(x)))


def _sigmoid(x):
    return 1.0 / (1.0 + jnp.exp(-x))


def _split_bf16(x):
    hi = x.astype(BF16)
    lo = (x - hi.astype(F32)).astype(BF16)
    return hi, lo


def _dot(a, b):
    return jnp.dot(a, b, preferred_element_type=F32)


def _head_sum(x, bd):
    hi, lo = _split_bf16(x)
    return _dot(hi, bd) + _dot(lo, bd)


def _head_sum_wide(x, bd):
    return jnp.concatenate(
        [_head_sum(x[:, j * LANES:(j + 1) * LANES], bd) for j in range(x.shape[1] // LANES)], axis=1)


def _proj_kernel(x_ref, g_ref, w_ref, o_ref, h_scr):
    @pl.when(pl.program_id(1) == 0)
    def _():
        x = x_ref[...]
        ms = jnp.mean(x * x, axis=-1, keepdims=True)
        h_scr[...] = (x * lax.rsqrt(ms + EPS) * g_ref[...]).astype(BF16)

    o_ref[...] = _dot(h_scr[...], w_ref[...])


def _proj(x2, g, w, tm, tn):
    n, d = x2.shape
    n_out = w.shape[1]
    return pl.pallas_call(
        _proj_kernel,
        grid=(n // tm, n_out // tn),
        in_specs=[pl.BlockSpec((tm, d), lambda i, j: (i, 0)),
                  pl.BlockSpec((1, d), lambda i, j: (0, 0)),
                  pl.BlockSpec((d, tn), lambda i, j: (0, j))],
        out_specs=pl.BlockSpec((tm, tn), lambda i, j: (i, j)),
        out_shape=jax.ShapeDtypeStruct((n, n_out), F32),
        scratch_shapes=[pltpu.VMEM((tm, d), BF16)],
        compiler_params=pltpu.CompilerParams(
            dimension_semantics=("parallel", "arbitrary"), vmem_limit_bytes=VMEM_LIMIT),
        name="proj",
    )(x2, g, w)


def _sb_kernel(*refs, t_q, qb, n_past):
    if n_past:
        (q_ref, k_ref, v_ref, kc_ref, vc_ref, mc_ref,
         o_ref, ko_ref, vo_ref, q2_scr, k2_scr, v2_scr, later_scr) = refs
    else:
        (q_ref, k_ref, v_ref, mc_ref,
         o_ref, ko_ref, vo_ref, q2_scr, k2_scr, v2_scr, later_scr) = refs

    kp = k_ref[...]
    vp = v_ref[...]
    ko_ref[0, 0] = kp[:, :D_HEAD]
    ko_ref[0, 1] = kp[:, D_HEAD:]
    vo_ref[0, 0] = vp[:, :D_HEAD]
    vo_ref[0, 1] = vp[:, D_HEAD:]
    q2_scr[...] = (q_ref[...] * (D_HEAD ** -0.5)).astype(BF16)

    lane = lax.broadcasted_iota(jnp.int32, (KEY_BLOCK, LANES), 1)
    first_head = lane < D_HEAD

    def put_block(j, kj, vj):
        k2_scr[j, :KEY_BLOCK] = jnp.where(first_head, kj, 0.0).astype(BF16)
        k2_scr[j, KEY_BLOCK:] = jnp.where(first_head, 0.0, kj).astype(BF16)
        v2_scr[j, :KEY_BLOCK] = jnp.where(first_head, vj, 0.0).astype(BF16)
        v2_scr[j, KEY_BLOCK:] = jnp.where(first_head, 0.0, vj).astype(BF16)

    if n_past:
        for j in range(n_past):
            rows = slice(j * KEY_BLOCK, (j + 1) * KEY_BLOCK)
            put_block(j,
                      jnp.concatenate([kc_ref[0, 0, rows, :], kc_ref[0, 1, rows, :]], axis=1),
                      jnp.concatenate([vc_ref[0, 0, rows, :], vc_ref[0, 1, rows, :]], axis=1))
        pad = jnp.zeros((KEY_BLOCK - t_q, LANES), F32)
        put_block(n_past, jnp.concatenate([kp, pad], axis=0), jnp.concatenate([vp, pad], axis=0))
    else:
        for j in range(t_q // KEY_BLOCK):
            rows = slice(j * KEY_BLOCK, (j + 1) * KEY_BLOCK)
            put_block(j, kp[rows], vp[rows])

    row = lax.broadcasted_iota(jnp.int32, (qb, 2 * KEY_BLOCK), 0)
    col = lax.broadcasted_iota(jnp.int32, (qb, 2 * KEY_BLOCK), 1) % KEY_BLOCK
    valid = col < row
    mc = mc_ref[...]

    def scores(qi, j):
        z = lax.dot_general(qi, k2_scr[j], (((1,), (1,)), ((), ())), preferred_element_type=F32)
        sp = _softplus(z)
        return z - sp, -sp

    def suffix_sums(lk):
        hi, lo = _split_bf16(lk)
        r = _dot(jnp.concatenate([hi, lo], axis=1), mc)
        return r[:, :2 * KEY_BLOCK] - lk, r[:, 2 * KEY_BLOCK:]

    def q_block(i, carry):
        qi = q2_scr[pl.ds(pl.multiple_of(i * qb, qb), qb), :]
        sj = n_past + i
        ls, lk = scores(qi, sj)
        lk = jnp.where(valid, lk, 0.0)
        excl, tot = suffix_sums(lk)
        w = jnp.where(valid, jnp.exp(ls + excl), 0.0)
        acc = _dot(w.astype(BF16), v2_scr[sj])
        later_scr[...] = tot

        def k_block(jj, acc):
            j = sj - 1 - jj
            ls, lk = scores(qi, j)
            excl, tot = suffix_sums(lk)
            later = later_scr[...]
            w = jnp.exp(ls + excl + later)
            later_scr[...] = later + tot
            return acc + _dot(w.astype(BF16), v2_scr[j])

        acc = lax.fori_loop(0, sj, k_block, acc)
        o_ref[pl.ds(pl.multiple_of(i * qb, qb), qb), :] = acc
        return carry

    lax.fori_loop(0, t_q // qb, q_block, 0)


def _cumsum_matrix():
    s = jnp.arange(KEY_BLOCK)
    incl = (s[:, None] >= s[None, :]).astype(F32)
    ones = jnp.ones((KEY_BLOCK, KEY_BLOCK), F32)
    z = jnp.zeros((KEY_BLOCK, KEY_BLOCK), F32)
    bd = lambda m: jnp.concatenate([jnp.concatenate([m, z], 1), jnp.concatenate([z, m], 1)], 0)
    r = jnp.concatenate([bd(incl), bd(ones)], axis=1)
    return jnp.concatenate([r, r], axis=0).astype(BF16)


def _sb(qkvg, k_past, v_past, batch, t_q):
    n = batch * t_q
    n_past = 0 if k_past is None else k_past.shape[2] // KEY_BLOCK
    qb = min(t_q, KEY_BLOCK)
    n_kb = n_past + 1 if n_past else t_q // KEY_BLOCK
    col = lambda off: pl.BlockSpec((t_q, LANES), lambda b, p: (b, off + p))
    in_specs = [col(0), col(N_PAIRS), col(2 * N_PAIRS)]
    args = [qkvg, qkvg, qkvg]
    if n_past:
        past = k_past.shape[2]
        cache = pl.BlockSpec((1, 2, past, D_HEAD), lambda b, p: (b, p, 0, 0))
        in_specs += [cache, cache]
        args += [k_past, v_past]
    in_specs.append(pl.BlockSpec((4 * KEY_BLOCK, 4 * KEY_BLOCK), lambda b, p: (0, 0)))
    args.append(_cumsum_matrix())
    kv_out = pl.BlockSpec((1, 2, t_q, D_HEAD), lambda b, p: (b, p, 0, 0))
    return pl.pallas_call(
        functools.partial(_sb_kernel, t_q=t_q, qb=qb, n_past=n_past),
        grid=(batch, N_PAIRS),
        in_specs=in_specs,
        out_specs=[pl.BlockSpec((t_q, LANES), lambda b, p: (b, p)), kv_out, kv_out],
        out_shape=[jax.ShapeDtypeStruct((n, D_MODEL), F32),
                   jax.ShapeDtypeStruct((batch, N_HEADS, t_q, D_HEAD), F32),
                   jax.ShapeDtypeStruct((batch, N_HEADS, t_q, D_HEAD), F32)],
        scratch_shapes=[pltpu.VMEM((t_q, LANES), BF16),
                        pltpu.VMEM((n_kb, 2 * KEY_BLOCK, LANES), BF16),
                        pltpu.VMEM((n_kb, 2 * KEY_BLOCK, LANES), BF16),
                        pltpu.VMEM((qb, 2 * KEY_BLOCK), F32)],
        compiler_params=pltpu.CompilerParams(
            dimension_semantics=("parallel", "parallel"), vmem_limit_bytes=VMEM_LIMIT),
        name="sb",
    )(*args)


def _rwkv_kernel(pb_ref, sh_ref, s0_ref, mu_ref, w0_ref, w2_ref, a0_ref, a2_ref, kk_ref, ka_ref,
                 rk_ref, lg_ref, lb_ref, bd_ref,
                 og_ref, so_ref,
                 s_scr, prev_scr, kk_s, w_s, be_s, k_s, r_s, v_s, bon_s, sz_s, ob_s, *, gb, chunk):
    c = pl.program_id(1)
    bd = bd_ref[...]

    @pl.when(c == 0)
    def _():
        for b in range(gb):
            prev_scr[b] = sh_ref[b]
            for j in range(N_PAIRS):
                s_scr[b, :, j * LANES:(j + 1) * LANES] = jnp.concatenate(
                    [s0_ref[b, 2 * j], s0_ref[b, 2 * j + 1]], axis=1)

    first_row = lax.broadcasted_iota(jnp.int32, (chunk, C_SHIFT), 0) == 0
    for b in range(gb):
        p = pb_ref[b]
        p_prev = jnp.where(first_row, prev_scr[b], pltpu.roll(p, 1, 0))
        prev_scr[b] = p[chunk - 1:chunk, :]
        pm = p + mu_ref[...] * (p_prev - p)
        r = pm[:, 0:D_MODEL]
        kb = pm[:, D_MODEL:2 * D_MODEL]
        vb = pm[:, 2 * D_MODEL:3 * D_MODEL]
        lora = pm[:, LORA_OFF:LORA_OFF + LANES]
        zb = pm[:, ZB_OFF:ZB_OFF + D_MODEL]
        w_raw = -_softplus(-(w0_ref[...] + _dot(jnp.tanh(lora).astype(BF16), w2_ref[...]))) - 0.5
        a = _sigmoid(a0_ref[...] + _dot(lora.astype(BF16), a2_ref[...]))
        kk = kb * kk_ref[...]
        norm = jnp.sqrt(_head_sum_wide(kk * kk, bd))
        kk = kk / jnp.maximum(norm, 1e-12)
        k2 = kb * (1.0 + (a - 1.0) * ka_ref[...])
        kk_s[b] = kk
        w_s[b] = jnp.exp(-jnp.exp(w_raw))
        be_s[b] = kk * a
        k_s[b] = k2
        r_s[b] = r
        v_s[b] = vb
        bon_s[b] = _head_sum_wide(r * k2 * rk_ref[...], bd) * vb
        sz_s[b] = zb * _sigmoid(zb)

    row = lax.broadcasted_iota(jnp.int32, (D_HEAD, LANES), 0)
    lane_in_head = lax.broadcasted_iota(jnp.int32, (D_HEAD, LANES), 1) % D_HEAD
    diag = lane_in_head == row

    def steps(t8, carry):
        t0 = pl.multiple_of(t8 * SUBLANES, SUBLANES)
        for b in range(gb):
            for j in range(N_PAIRS):
                sl = slice(j * LANES, (j + 1) * LANES)
                tile = lambda ref: ref[b, pl.ds(t0, SUBLANES), sl]
                kk_t, w_t, be_t, k_t, r_t, v_t = (tile(ref) for ref in (kk_s, w_s, be_s, k_s, r_s, v_s))
                s = s_scr[b, :, sl]
                ob = ob_s[b, :, sl]
                for u in range(SUBLANES):
                    vec = lambda x: x[u:u + 1, :]
                    lhs = jnp.concatenate([s * vec(kk_t), jnp.where(diag, vec(v_t), 0.0)], axis=0)
                    res = _head_sum(lhs, bd)
                    sa = res[:D_HEAD]
                    v_col = res[D_HEAD:]
                    s = s * vec(w_t) - sa * vec(be_t) + v_col * vec(k_t)
                    o_col = _head_sum(s * vec(r_t), bd)
                    ob = jnp.where(lane_in_head == t0 + u, o_col, ob)
                s_scr[b, :, sl] = s
                ob_s[b, :, sl] = ob
        return carry

    lax.fori_loop(0, chunk // SUBLANES, steps, 0)

    zero_rows = jnp.zeros((LANES - D_HEAD, LANES), F32)
    for b in range(gb):
        tiles = []
        for j in range(N_PAIRS):
            tr = jnp.concatenate([ob_s[b, :, j * LANES:(j + 1) * LANES], zero_rows], axis=0).T
            tiles.append(jnp.concatenate(
                [tr[0:chunk, 0:D_HEAD], tr[D_HEAD:D_HEAD + chunk, 0:D_HEAD]], axis=1))
        o = jnp.concatenate(tiles, axis=1)
        mean = _head_sum_wide(o, bd) * (1.0 / D_HEAD)
        dlt = o - mean
        var = _head_sum_wide(dlt * dlt, bd) * (1.0 / D_HEAD)
        o = dlt * lax.rsqrt(var + LNX_EPS) * lg_ref[...] + lb_ref[...] + bon_s[b]
        og_ref[b] = o * sz_s[b]

    @pl.when(c == pl.num_programs(1) - 1)
    def _():
        for b in range(gb):
            for h in range(N_HEADS):
                so_ref[b, h] = s_scr[b, :, h * D_HEAD:(h + 1) * D_HEAD]


def _rwkv(pb, shift_prev, wkv_prev, prm, gb, chunk):
    batch, t, _ = pb.shape
    row = lambda v: v.reshape(1, -1)
    const = lambda shape: pl.BlockSpec(shape, lambda g, c: (0,) * len(shape))
    zeros = jnp.zeros((D_HEAD, D_MODEL), F32)
    w2p = jnp.concatenate([prm["w2"], zeros], axis=0).astype(BF16)
    a2p = jnp.concatenate([zeros, prm["a2"]], axis=0).astype(BF16)
    head = jnp.arange(LANES) // D_HEAD
    bd = (head[:, None] == head[None, :]).astype(BF16)
    vec_spec = const((1, D_MODEL))
    act = lambda: pltpu.VMEM((gb, chunk, D_MODEL), F32)
    return pl.pallas_call(
        functools.partial(_rwkv_kernel, gb=gb, chunk=chunk),
        grid=(batch // gb, t // chunk),
        in_specs=[pl.BlockSpec((gb, chunk, C_SHIFT), lambda g, c: (g, c, 0)),
                  pl.BlockSpec((gb, 1, C_SHIFT), lambda g, c: (g, 0, 0)),
                  pl.BlockSpec((gb, N_HEADS, D_HEAD, D_HEAD), lambda g, c: (g, 0, 0, 0)),
                  const((1, C_SHIFT)), vec_spec, const((LANES, D_MODEL)), vec_spec,
                  const((LANES, D_MODEL)), vec_spec, vec_spec, vec_spec, vec_spec, vec_spec,
                  const((LANES, LANES))],
        out_specs=[pl.BlockSpec((gb, chunk, D_MODEL), lambda g, c: (g, c, 0)),
                   pl.BlockSpec((gb, N_HEADS, D_HEAD, D_HEAD), lambda g, c: (g, 0, 0, 0))],
        out_shape=[jax.ShapeDtypeStruct((batch, t, D_MODEL), F32),
                   jax.ShapeDtypeStruct((batch, N_HEADS, D_HEAD, D_HEAD), F32)],
        scratch_shapes=[pltpu.VMEM((gb, D_HEAD, D_MODEL), F32),
                        pltpu.VMEM((gb, 1, C_SHIFT), F32),
                        act(), act(), act(), act(), act(), act(), act(), act(),
                        pltpu.VMEM((gb, D_HEAD, D_MODEL), F32)],
        compiler_params=pltpu.CompilerParams(
            dimension_semantics=("parallel", "arbitrary"), vmem_limit_bytes=VMEM_LIMIT),
        name="rwkv",
    )(pb, shift_prev, wkv_prev, row(prm["mu_shift"]), row(prm["w0"]), w2p, row(prm["a0"]), a2p,
      row(prm["k_k"]), row(prm["k_a"]), row(prm["r_k"]), row(prm["lnx_g"]), row(prm["lnx_b"]), bd)


def _out_kernel(x_ref, za_ref, ga_ref, gb_ref, oa_ref, ob_ref, woa_ref, wob_ref, wout_ref, fg_ref,
                y_ref, *, final_norm):
    za = za_ref[...]
    y_a = _dot((oa_ref[...] * (za * _sigmoid(za))).astype(BF16), woa_ref[...])
    y_b = _dot(ob_ref[...].astype(BF16), wob_ref[...])
    merged = _sigmoid(ga_ref[...]) * y_a + _sigmoid(gb_ref[...]) * y_b
    y = x_ref[...] + _dot(merged.astype(BF16), wout_ref[...])
    if final_norm:
        ms = jnp.mean(y * y, axis=-1, keepdims=True)
        y = y * lax.rsqrt(ms + EPS) * fg_ref[...]
    y_ref[...] = y


def _out(x2, qkvg, o_a, og_b, w_o_a, w_o_b, w_out, final_g, tm, final_norm):
    n, d = x2.shape
    tok = lambda blk: pl.BlockSpec((tm, d), lambda i: (i, blk))
    wspec = pl.BlockSpec((d, d), lambda i: (0, 0))
    return pl.pallas_call(
        functools.partial(_out_kernel, final_norm=final_norm),
        grid=(n // tm,),
        in_specs=[tok(0), tok(3), tok(4), tok(5), tok(0), tok(0), wspec, wspec, wspec,
                  pl.BlockSpec((1, d), lambda i: (0, 0))],
        out_specs=tok(0),
        out_shape=jax.ShapeDtypeStruct((n, d), F32),
        compiler_params=pltpu.CompilerParams(
            dimension_semantics=("parallel",), vmem_limit_bytes=VMEM_LIMIT),
        name="out",
    )(x2, qkvg, qkvg, qkvg, o_a, og_b, w_o_a, w_o_b, w_out, final_g)


def _layer(x, shift_prev, k_past, v_past, wkv_prev, prm, final_g, final_norm, gb, chunk):
    batch, t, d = x.shape
    n = batch * t
    x2 = x.reshape(n, d)
    g = prm["norm_g"].reshape(1, d)
    tm = min(n, 512)
    qkvg = _proj(x2, g, prm["w_in_a"], tm, 2 * D_MODEL)
    pb = _proj(x2, g, prm["w_in_b"], tm, C_SHIFT // 3).reshape(batch, t, C_SHIFT)
    o_a, k_new, v_new = _sb(qkvg, k_past, v_past, batch, t)
    og_b, wkv_new = _rwkv(pb, shift_prev, wkv_prev, prm, gb, chunk)
    y = _out(x2, qkvg, o_a, og_b.reshape(n, d), prm["w_o_a"], prm["w_o_b"], prm["w_out"],
             final_g.reshape(1, d), min(n, 256), final_norm)
    return y.reshape(batch, t, d), k_new, v_new, pb[:, t - 1:, :], wkv_new


def kernel(x_prompt, x_sample, cache_sb_k, cache_sb_v, state_shift, state_wkv, norm_g, w_in, mu_shift,
           w0, w2, a0, a2, k_k, k_a, r_k, lnx_g, lnx_b, w_o_a, w_o_b, w_out, final_norm_g):
    depth = w_in.shape[0]
    batch = x_prompt.shape[0]
    xp, xs = x_prompt, x_sample
    outs_p, outs_s = [], []
    for l in range(depth):
        prm = dict(norm_g=norm_g[l],
                   w_in_a=w_in[l][:, :C_QKVG].astype(BF16), w_in_b=w_in[l][:, C_QKVG:].astype(BF16),
                   mu_shift=mu_shift[l], w0=w0[l], w2=w2[l], a0=a0[l], a2=a2[l], k_k=k_k[l], k_a=k_a[l],
                   r_k=r_k[l].reshape(-1), lnx_g=lnx_g[l], lnx_b=lnx_b[l],
                   w_o_a=w_o_a[l].astype(BF16), w_o_b=w_o_b[l].astype(BF16), w_out=w_out[l].astype(BF16))
        last = l == depth - 1
        zero_shift = jnp.zeros((batch, 1, C_SHIFT), F32)
        zero_wkv = jnp.zeros((batch, N_HEADS, D_HEAD, D_HEAD), F32)
        xp, *rest_p = _layer(xp, zero_shift, None, None, zero_wkv, prm, final_norm_g, last,
                             gb=RWKV_GROUP, chunk=RWKV_CHUNK)
        xs, *rest_s = _layer(xs, state_shift[l], cache_sb_k[l], cache_sb_v[l], state_wkv[l],
                             prm, final_norm_g, last, gb=RWKV_GROUP, chunk=x_sample.shape[1])
        outs_p.append(rest_p)
        outs_s.append(rest_s)
    stack = lambda outs, i: jnp.stack([o[i] for o in outs])
    return (xp, xs,
            stack(outs_p, 0), stack(outs_p, 1), stack(outs_p, 2), stack(outs_p, 3),
            stack(outs_s, 0), stack(outs_s, 1), stack(outs_s, 2), stack(outs_s, 3))
```

```python
import functools

import jax
import jax.numpy as jnp
from jax import lax
from jax.experimental import pallas as pl
from jax.experimental.pallas import tpu as pltpu

F32 = jnp.float32
BF16 = jnp.bfloat16

D_MODEL = 1024
N_HEADS = 16
D_HEAD = 64
LANES = 128
SUBLANES = 8
N_PAIRS = N_HEADS // 2
C_QKVG = 6 * D_MODEL
C_SHIFT = 3 * D_MODEL + 2 * D_HEAD + D_MODEL
LORA_OFF = 3 * D_MODEL
ZB_OFF = LORA_OFF + LANES
EPS = 1e-6
LNX_EPS = 64e-5
KEY_BLOCK = 128
RWKV_GROUP = 1
RWKV_CHUNK = 64
VMEM_LIMIT = 56 * 1024 * 1024


def _softplus(x):
    return jnp.maximum(x, 0.0) + jnp.log1p(jnp.exp(-jnp.abs(x)))


def _sigmoid(x):
    return 1.0 / (1.0 + jnp.exp(-x))


def _split_bf16(x):
    hi = x.astype(BF16)
    lo = (x - hi.astype(F32)).astype(BF16)
    return hi, lo


def _dot(a, b):
    return jnp.dot(a, b, preferred_element_type=F32)


def _head_sum(x, bd):
    hi, lo = _split_bf16(x)
    return _dot(hi, bd) + _dot(lo, bd)


def _head_sum_wide(x, bd):
    return jnp.concatenate(
        [_head_sum(x[:, j * LANES:(j + 1) * LANES], bd) for j in range(x.shape[1] // LANES)], axis=1)


def _proj_kernel(x_ref, g_ref, w_ref, o_ref, h_scr):
    @pl.when(pl.program_id(1) == 0)
    def _():
        x = x_ref[...]
        ms = jnp.mean(x * x, axis=-1, keepdims=True)
        h_scr[...] = (x * lax.rsqrt(ms + EPS) * g_ref[...]).astype(BF16)

    o_ref[...] = _dot(h_scr[...], w_ref[...])


def _proj(x2, g, w, tm, tn):
    n, d = x2.shape
    n_out = w.shape[1]
    return pl.pallas_call(
        _proj_kernel,
        grid=(n // tm, n_out // tn),
        in_specs=[pl.BlockSpec((tm, d), lambda i, j: (i, 0)),
                  pl.BlockSpec((1, d), lambda i, j: (0, 0)),
                  pl.BlockSpec((d, tn), lambda i, j: (0, j))],
        out_specs=pl.BlockSpec((tm, tn), lambda i, j: (i, j)),
        out_shape=jax.ShapeDtypeStruct((n, n_out), F32),
        scratch_shapes=[pltpu.VMEM((tm, d), BF16)],
        compiler_params=pltpu.CompilerParams(
            dimension_semantics=("parallel", "arbitrary"), vmem_limit_bytes=VMEM_LIMIT),
        name="proj",
    )(x2, g, w)


def _sb_kernel(*refs, t_q, qb, n_past):
    if n_past:
        (q_ref, k_ref, v_ref, kc_ref, vc_ref, mc_ref,
         o_ref, ko_ref, vo_ref, q2_scr, k2_scr, v2_scr, later_scr) = refs
    else:
        (q_ref, k_ref, v_ref, mc_ref,
         o_ref, ko_ref, vo_ref, q2_scr, k2_scr, v2_scr, later_scr) = refs

    kp = k_ref[...]
    vp = v_ref[...]
    ko_ref[0, 0] = kp[:, :D_HEAD]
    ko_ref[0, 1] = kp[:, D_HEAD:]
    vo_ref[0, 0] = vp[:, :D_HEAD]
    vo_ref[0, 1] = vp[:, D_HEAD:]
    q2_scr[...] = (q_ref[...] * (D_HEAD ** -0.5)).astype(BF16)

    lane = lax.broadcasted_iota(jnp.int32, (KEY_BLOCK, LANES), 1)
    first_head = lane < D_HEAD

    def put_block(j, kj, vj):
        k2_scr[j, :KEY_BLOCK] = jnp.where(first_head, kj, 0.0).astype(BF16)
        k2_scr[j, KEY_BLOCK:] = jnp.where(first_head, 0.0, kj).astype(BF16)
        v2_scr[j, :KEY_BLOCK] = jnp.where(first_head, vj, 0.0).astype(BF16)
        v2_scr[j, KEY_BLOCK:] = jnp.where(first_head, 0.0, vj).astype(BF16)

    if n_past:
        for j in range(n_past):
            rows = slice(j * KEY_BLOCK, (j + 1) * KEY_BLOCK)
            put_block(j,
                      jnp.concatenate([kc_ref[0, 0, rows, :], kc_ref[0, 1, rows, :]], axis=1),
                      jnp.concatenate([vc_ref[0, 0, rows, :], vc_ref[0, 1, rows, :]], axis=1))
        pad = jnp.zeros((KEY_BLOCK - t_q, LANES), F32)
        put_block(n_past, jnp.concatenate([kp, pad], axis=0), jnp.concatenate([vp, pad], axis=0))
    else:
        for j in range(t_q // KEY_BLOCK):
            rows = slice(j * KEY_BLOCK, (j + 1) * KEY_BLOCK)
            put_block(j, kp[rows], vp[rows])

    row = lax.broadcasted_iota(jnp.int32, (qb, 2 * KEY_BLOCK), 0)
    col = lax.broadcasted_iota(jnp.int32, (qb, 2 * KEY_BLOCK), 1) % KEY_BLOCK
    valid = col < row
    mc = mc_ref[...]

    def scores(qi, j):
        z = lax.dot_general(qi, k2_scr[j], (((1,), (1,)), ((), ())), preferred_element_type=F32)
        sp = _softplus(z)
        return z - sp, -sp

    def suffix_sums(lk):
        hi, lo = _split_bf16(lk)
        r = _dot(jnp.concatenate([hi, lo], axis=1), mc)
        return r[:, :2 * KEY_BLOCK] - lk, r[:, 2 * KEY_BLOCK:]

    def q_block(i, carry):
        qi = q2_scr[pl.ds(pl.multiple_of(i * qb, qb), qb), :]
        sj = n_past + i
        ls, lk = scores(qi, sj)
        lk = jnp.where(valid, lk, 0.0)
        excl, tot = suffix_sums(lk)
        w = jnp.where(valid, jnp.exp(ls + excl), 0.0)
        acc = _dot(w.astype(BF16), v2_scr[sj])
        later_scr[...] = tot

        def k_block(jj, acc):
            j = sj - 1 - jj
            ls, lk = scores(qi, j)
            excl, tot = suffix_sums(lk)
            later = later_scr[...]
            w = jnp.exp(ls + excl + later)
            later_scr[...] = later + tot
            return acc + _dot(w.astype(BF16), v2_scr[j])

        acc = lax.fori_loop(0, sj, k_block, acc)
        o_ref[pl.ds(pl.multiple_of(i * qb, qb), qb), :] = acc
        return carry

    lax.fori_loop(0, t_q // qb, q_block, 0)


def _cumsum_matrix():
    s = jnp.arange(KEY_BLOCK)
    incl = (s[:, None] >= s[None, :]).astype(F32)
    ones = jnp.ones((KEY_BLOCK, KEY_BLOCK), F32)
    z = jnp.zeros((KEY_BLOCK, KEY_BLOCK), F32)
    bd = lambda m: jnp.concatenate([jnp.concatenate([m, z], 1), jnp.concatenate([z, m], 1)], 0)
    r = jnp.concatenate([bd(incl), bd(ones)], axis=1)
    return jnp.concatenate([r, r], axis=0).astype(BF16)


def _sb(qkvg, k_past, v_past, batch, t_q):
    n = batch * t_q
    n_past = 0 if k_past is None else k_past.shape[2] // KEY_BLOCK
    qb = min(t_q, KEY_BLOCK)
    n_kb = n_past + 1 if n_past else t_q // KEY_BLOCK
    col = lambda off: pl.BlockSpec((t_q, LANES), lambda b, p: (b, off + p))
    in_specs = [col(0), col(N_PAIRS), col(2 * N_PAIRS)]
    args = [qkvg, qkvg, qkvg]
    if n_past:
        past = k_past.shape[2]
        cache = pl.BlockSpec((1, 2, past, D_HEAD), lambda b, p: (b, p, 0, 0))
        in_specs += [cache, cache]
        args += [k_past, v_past]
    in_specs.append(pl.BlockSpec((4 * KEY_BLOCK, 4 * KEY_BLOCK), lambda b, p: (0, 0)))
    args.append(_cumsum_matrix())
    kv_out = pl.BlockSpec((1, 2, t_q, D_HEAD), lambda b, p: (b, p, 0, 0))
    return pl.pallas_call(
        functools.partial(_sb_kernel, t_q=t_q, qb=qb, n_past=n_past),
        grid=(batch, N_PAIRS),
        in_specs=in_specs,
        out_specs=[pl.BlockSpec((t_q, LANES), lambda b, p: (b, p)), kv_out, kv_out],
        out_shape=[jax.ShapeDtypeStruct((n, D_MODEL), F32),
                   jax.ShapeDtypeStruct((batch, N_HEADS, t_q, D_HEAD), F32),
                   jax.ShapeDtypeStruct((batch, N_HEADS, t_q, D_HEAD), F32)],
        scratch_shapes=[pltpu.VMEM((t_q, LANES), BF16),
                        pltpu.VMEM((n_kb, 2 * KEY_BLOCK, LANES), BF16),
                        pltpu.VMEM((n_kb, 2 * KEY_BLOCK, LANES), BF16),
                        pltpu.VMEM((qb, 2 * KEY_BLOCK), F32)],
        compiler_params=pltpu.CompilerParams(
            dimension_semantics=("parallel", "parallel"), vmem_limit_bytes=VMEM_LIMIT),
        name="sb",
    )(*args)


def _rwkv_kernel(pb_ref, sh_ref, s0_ref, mu_ref, w0_ref, w2_ref, a0_ref, a2_ref, kk_ref, ka_ref,
                 rk_ref, lg_ref, lb_ref, bd_ref, tri_ref,
                 og_ref, so_ref, s_scr, prev_scr, *, gb, rows):
    ch = RWKV_CHUNK
    c = pl.program_id(1)
    bd = bd_ref[...]
    tri = tri_ref[...]

    lane = lax.broadcasted_iota(jnp.int32, (ch, LANES), 1)
    row = lax.broadcasted_iota(jnp.int32, (ch, LANES), 0)
    first_head = lane < D_HEAD
    t_in = lane % D_HEAD
    strict = t_in < row
    incl = t_in <= row
    eye2 = jnp.where(t_in == row, 1.0, 0.0)
    same_head = (lax.broadcasted_iota(jnp.int32, (LANES, LANES), 0) // D_HEAD
                 == lax.broadcasted_iota(jnp.int32, (LANES, LANES), 1) // D_HEAD)

    def blockdiag(y):
        return jnp.concatenate([jnp.where(first_head, y, 0.0), jnp.where(first_head, 0.0, y)],
                               axis=0).astype(BF16)

    def pair_dot(x, y):
        return _dot(x.astype(BF16), blockdiag(y))

    def dot_nt(x, y):
        return lax.dot_general(x, y, (((1,), (1,)), ((), ())), preferred_element_type=F32)

    @pl.when(c == 0)
    def _():
        zero = jnp.zeros((D_HEAD, D_HEAD), F32)
        for b in range(gb):
            prev_scr[b] = sh_ref[b]
            for j in range(N_PAIRS):
                s_scr[b, j] = jnp.concatenate(
                    [jnp.concatenate([s0_ref[b, 2 * j], zero], axis=1),
                     jnp.concatenate([zero, s0_ref[b, 2 * j + 1]], axis=1)], axis=0)

    wide_row = lax.broadcasted_iota(jnp.int32, (ch, C_SHIFT), 0)
    live = lax.broadcasted_iota(jnp.int32, (ch, D_MODEL), 0) < rows
    for b in range(gb):
        p = pb_ref[b]
        if rows < ch:
            p = jnp.concatenate([p, jnp.zeros((ch - rows, C_SHIFT), F32)], axis=0)
        p_prev = jnp.where(wide_row == 0, prev_scr[b], pltpu.roll(p, 1, 0))
        prev_scr[b] = p[rows - 1:rows, :]
        pm = p + mu_ref[...] * (p_prev - p)
        r = pm[:, 0:D_MODEL]
        kb = pm[:, D_MODEL:2 * D_MODEL]
        vb = pm[:, 2 * D_MODEL:3 * D_MODEL]
        lora = pm[:, LORA_OFF:LORA_OFF + LANES]
        zb = pm[:, ZB_OFF:ZB_OFF + D_MODEL]
        w_raw = -_softplus(-(w0_ref[...] + _dot(jnp.tanh(lora).astype(BF16), w2_ref[...]))) - 0.5
        lw = -jnp.exp(w_raw)
        a = _sigmoid(a0_ref[...] + _dot(lora.astype(BF16), a2_ref[...]))
        kk = kb * kk_ref[...]
        norm = jnp.sqrt(_head_sum_wide(kk * kk, bd))
        kk = kk / jnp.maximum(norm, 1e-12)
        k2 = kb * (1.0 + (a - 1.0) * ka_ref[...])
        if rows < ch:
            r, kk, k2, vb, lw = (jnp.where(live, x, 0.0) for x in (r, kk, k2, vb, lw))
        be = kk * a
        bonus = _head_sum_wide(r * k2 * rk_ref[...], bd) * vb
        l1 = lw.astype(BF16)
        rem = lw - l1.astype(F32)
        l2 = rem.astype(BF16)
        l3 = (rem - l2.astype(F32)).astype(BF16)
        cs = _dot(tri, l1) + _dot(tri, l2) + _dot(tri, l3)

        pairs = range(N_PAIRS)
        tile = lambda arr, j: arr[:, j * LANES:(j + 1) * LANES]
        cs_end = [tile(cs, j)[ch - 1:ch, :] for j in pairs]
        g_inv = [jnp.exp(-tile(cs, j)) for j in pairs]
        x = [jnp.concatenate([tile(kk, j) * jnp.exp(tile(cs, j) - tile(lw, j)),
                              tile(r, j) * jnp.exp(tile(cs, j))], axis=0).astype(BF16) for j in pairs]
        gram = [dot_nt(x[j], jnp.concatenate([blockdiag(tile(be, j) * g_inv[j]),
                                              blockdiag(tile(k2, j) * g_inv[j])], axis=0)) for j in pairs]
        l_ab = [jnp.where(strict, gram[j][:ch, :LANES], 0.0) for j in pairs]
        m_ak = [jnp.where(strict, gram[j][:ch, LANES:], 0.0).astype(BF16) for j in pairs]
        m_r = [jnp.concatenate([jnp.where(incl, gram[j][ch:, :LANES], 0.0),
                                jnp.where(incl, gram[j][ch:, LANES:], 0.0)], axis=1).astype(BF16)
               for j in pairs]
        inv = [eye2 - l_ab[j] for j in pairs]
        pw = [pair_dot(l_ab[j], l_ab[j]) for j in pairs]
        n_prod = ch.bit_length() - 2
        for it in range(n_prod):
            inv = [inv[j] + pair_dot(inv[j], pw[j]) for j in pairs]
            if it < n_prod - 1:
                pw = [pair_dot(pw[j], pw[j]) for j in pairs]
        s_bd = [s_scr[b, j] for j in pairs]
        w1 = [dot_nt(x[j], s_bd[j].astype(BF16)) for j in pairs]
        v_bd = [blockdiag(tile(vb, j)) for j in pairs]
        z = [w1[j][:ch] + _dot(m_ak[j], v_bd[j]) for j in pairs]
        u = [-pair_dot(inv[j], z[j]) for j in pairs]
        outs = [w1[j][ch:] + _dot(m_r[j], jnp.concatenate([blockdiag(u[j]), v_bd[j]], axis=0))
                for j in pairs]
        for j in pairs:
            g_end = jnp.exp(cs_end[j] - tile(cs, j))
            uv_t = jnp.concatenate([u[j], tile(vb, j)], axis=0).T.astype(BF16)
            bk = jnp.concatenate([tile(be, j) * g_end, tile(k2, j) * g_end], axis=0).astype(BF16)
            s_scr[b, j] = s_bd[j] * jnp.exp(cs_end[j]) + jnp.where(same_head, _dot(uv_t, bk), 0.0)

        o = jnp.concatenate(outs, axis=1)
        mean = _head_sum_wide(o, bd) * (1.0 / D_HEAD)
        dlt = o - mean
        var = _head_sum_wide(dlt * dlt, bd) * (1.0 / D_HEAD)
        o = dlt * lax.rsqrt(var + LNX_EPS) * lg_ref[...] + lb_ref[...] + bonus
        og_ref[b] = (o * (zb * _sigmoid(zb)))[:rows]

    @pl.when(c == pl.num_programs(1) - 1)
    def _():
        for b in range(gb):
            for j in range(N_PAIRS):
                so_ref[b, 2 * j] = s_scr[b, j, :D_HEAD, :D_HEAD]
                so_ref[b, 2 * j + 1] = s_scr[b, j, D_HEAD:, D_HEAD:]


def _rwkv(pb, shift_prev, wkv_prev, prm, gb):
    batch, t, _ = pb.shape
    rows = min(t, RWKV_CHUNK)
    row = lambda v: v.reshape(1, -1)
    const = lambda shape: pl.BlockSpec(shape, lambda g, c: (0,) * len(shape))
    zeros = jnp.zeros((D_HEAD, D_MODEL), F32)
    w2p = jnp.concatenate([prm["w2"], zeros], axis=0).astype(BF16)
    a2p = jnp.concatenate([zeros, prm["a2"]], axis=0).astype(BF16)
    head = jnp.arange(LANES) // D_HEAD
    bd = (head[:, None] == head[None, :]).astype(BF16)
    tok = jnp.arange(RWKV_CHUNK)
    tri = (tok[:, None] >= tok[None, :]).astype(BF16)
    vec_spec = const((1, D_MODEL))
    return pl.pallas_call(
        functools.partial(_rwkv_kernel, gb=gb, rows=rows),
        grid=(batch // gb, t // rows),
        in_specs=[pl.BlockSpec((gb, rows, C_SHIFT), lambda g, c: (g, c, 0)),
                  pl.BlockSpec((gb, 1, C_SHIFT), lambda g, c: (g, 0, 0)),
                  pl.BlockSpec((gb, N_HEADS, D_HEAD, D_HEAD), lambda g, c: (g, 0, 0, 0)),
                  const((1, C_SHIFT)), vec_spec, const((LANES, D_MODEL)), vec_spec,
                  const((LANES, D_MODEL)), vec_spec, vec_spec, vec_spec, vec_spec, vec_spec,
                  const((LANES, LANES)), const((RWKV_CHUNK, RWKV_CHUNK))],
        out_specs=[pl.BlockSpec((gb, rows, D_MODEL), lambda g, c: (g, c, 0)),
                   pl.BlockSpec((gb, N_HEADS, D_HEAD, D_HEAD), lambda g, c: (g, 0, 0, 0))],
        out_shape=[jax.ShapeDtypeStruct((batch, t, D_MODEL), F32),
                   jax.ShapeDtypeStruct((batch, N_HEADS, D_HEAD, D_HEAD), F32)],
        scratch_shapes=[pltpu.VMEM((gb, N_PAIRS, LANES, LANES), F32),
                        pltpu.VMEM((gb, 1, C_SHIFT), F32)],
        compiler_params=pltpu.CompilerParams(
            dimension_semantics=("parallel", "arbitrary"), vmem_limit_bytes=VMEM_LIMIT),
        name="rwkv",
    )(pb, shift_prev, wkv_prev, row(prm["mu_shift"]), row(prm["w0"]), w2p, row(prm["a0"]), a2p,
      row(prm["k_k"]), row(prm["k_a"]), row(prm["r_k"]), row(prm["lnx_g"]), row(prm["lnx_b"]), bd, tri)


def _out_kernel(x_ref, za_ref, ga_ref, gb_ref, oa_ref, ob_ref, woa_ref, wob_ref, wout_ref, fg_ref,
                y_ref, *, final_norm):
    za = za_ref[...]
    y_a = _dot((oa_ref[...] * (za * _sigmoid(za))).astype(BF16), woa_ref[...])
    y_b = _dot(ob_ref[...].astype(BF16), wob_ref[...])
    merged = _sigmoid(ga_ref[...]) * y_a + _sigmoid(gb_ref[...]) * y_b
    y = x_ref[...] + _dot(merged.astype(BF16), wout_ref[...])
    if final_norm:
        ms = jnp.mean(y * y, axis=-1, keepdims=True)
        y = y * lax.rsqrt(ms + EPS) * fg_ref[...]
    y_ref[...] = y


def _out(x2, qkvg, o_a, og_b, w_o_a, w_o_b, w_out, final_g, tm, final_norm):
    n, d = x2.shape
    tok = lambda blk: pl.BlockSpec((tm, d), lambda i: (i, blk))
    wspec = pl.BlockSpec((d, d), lambda i: (0, 0))
    return pl.pallas_call(
        functools.partial(_out_kernel, final_norm=final_norm),
        grid=(n // tm,),
        in_specs=[tok(0), tok(3), tok(4), tok(5), tok(0), tok(0), wspec, wspec, wspec,
                  pl.BlockSpec((1, d), lambda i: (0, 0))],
        out_specs=tok(0),
        out_shape=jax.ShapeDtypeStruct((n, d), F32),
        compiler_params=pltpu.CompilerParams(
            dimension_semantics=("parallel",), vmem_limit_bytes=VMEM_LIMIT),
        name="out",
    )(x2, qkvg, qkvg, qkvg, o_a, og_b, w_o_a, w_o_b, w_out, final_g)


def _layer(x, shift_prev, k_past, v_past, wkv_prev, prm, final_g, final_norm):
    batch, t, d = x.shape
    n = batch * t
    x2 = x.reshape(n, d)
    g = prm["norm_g"].reshape(1, d)
    tm = min(n, 512)
    qkvg = _proj(x2, g, prm["w_in_a"], tm, 2 * D_MODEL)
    pb = _proj(x2, g, prm["w_in_b"], tm, C_SHIFT // 3).reshape(batch, t, C_SHIFT)
    o_a, k_new, v_new = _sb(qkvg, k_past, v_past, batch, t)
    og_b, wkv_new = _rwkv(pb, shift_prev, wkv_prev, prm, RWKV_GROUP)
    y = _out(x2, qkvg, o_a, og_b.reshape(n, d), prm["w_o_a"], prm["w_o_b"], prm["w_out"],
             final_g.reshape(1, d), min(n, 256), final_norm)
    return y.reshape(batch, t, d), k_new, v_new, pb[:, t - 1:, :], wkv_new


def kernel(x_prompt, x_sample, cache_sb_k, cache_sb_v, state_shift, state_wkv, norm_g, w_in, mu_shift,
           w0, w2, a0, a2, k_k, k_a, r_k, lnx_g, lnx_b, w_o_a, w_o_b, w_out, final_norm_g):
    depth = w_in.shape[0]
    batch = x_prompt.shape[0]
    xp, xs = x_prompt, x_sample
    outs_p, outs_s = [], []
    for l in range(depth):
        prm = dict(norm_g=norm_g[l],
                   w_in_a=w_in[l][:, :C_QKVG].astype(BF16), w_in_b=w_in[l][:, C_QKVG:].astype(BF16),
                   mu_shift=mu_shift[l], w0=w0[l], w2=w2[l], a0=a0[l], a2=a2[l], k_k=k_k[l], k_a=k_a[l],
                   r_k=r_k[l].reshape(-1), lnx_g=lnx_g[l], lnx_b=lnx_b[l],
                   w_o_a=w_o_a[l].astype(BF16), w_o_b=w_o_b[l].astype(BF16), w_out=w_out[l].astype(BF16))
        last = l == depth - 1
        zero_shift = jnp.zeros((batch, 1, C_SHIFT), F32)
        zero_wkv = jnp.zeros((batch, N_HEADS, D_HEAD, D_HEAD), F32)
        xp, *rest_p = _layer(xp, zero_shift, None, None, zero_wkv, prm, final_norm_g, last)
        xs, *rest_s = _layer(xs, state_shift[l], cache_sb_k[l], cache_sb_v[l], state_wkv[l],
                             prm, final_norm_g, last)
        outs_p.append(rest_p)
        outs_s.append(rest_s)
    stack = lambda outs, i: jnp.stack([o[i] for o in outs])
    return (xp, xs,
            stack(outs_p, 0), stack(outs_p, 1), stack(outs_p, 2), stack(outs_p, 3),
            stack(outs_s, 0), stack(outs_s, 1), stack(outs_s, 2), stack(outs_s, 3))
```

```python
import functools

import jax
import jax.numpy as jnp
from jax import lax
from jax.experimental import pallas as pl
from jax.experimental.pallas import tpu as pltpu

F32 = jnp.float32
BF16 = jnp.bfloat16

D_MODEL = 1024
N_HEADS = 16
D_HEAD = 64
LANES = 128
SUBLANES = 8
N_PAIRS = N_HEADS // 2
C_QKVG = 6 * D_MODEL
C_SHIFT = 3 * D_MODEL + 2 * D_HEAD + D_MODEL
LORA_OFF = 3 * D_MODEL
ZB_OFF = LORA_OFF + LANES
EPS = 1e-6
LNX_EPS = 64e-5
KEY_BLOCK = 128
SB_Q_BLOCK = 256
EXP_ZERO = -110.0
RWKV_GROUP = 1
RWKV_CHUNK = 64
VMEM_LIMIT = 56 * 1024 * 1024


def _softplus(x):
    return jnp.maximum(x, 0.0) + jnp.log1p(jnp.exp(-jnp.abs(x)))


def _sigmoid(x):
    return 1.0 / (1.0 + jnp.exp(-x))


def _split_bf16(x):
    hi = x.astype(BF16)
    lo = (x - hi.astype(F32)).astype(BF16)
    return hi, lo


def _dot(a, b):
    return jnp.dot(a, b, preferred_element_type=F32)


def _head_sum(x, bd):
    hi, lo = _split_bf16(x)
    return _dot(hi, bd) + _dot(lo, bd)


def _head_sum_wide(x, bd):
    return jnp.concatenate(
        [_head_sum(x[:, j * LANES:(j + 1) * LANES], bd) for j in range(x.shape[1] // LANES)], axis=1)


def _proj_kernel(x_ref, g_ref, w_ref, o_ref, h_scr):
    @pl.when(pl.program_id(1) == 0)
    def _():
        x = x_ref[...]
        ms = jnp.mean(x * x, axis=-1, keepdims=True)
        h_scr[...] = (x * lax.rsqrt(ms + EPS) * g_ref[...]).astype(BF16)

    o_ref[...] = _dot(h_scr[...], w_ref[...])


def _proj(x2, g, w, tm, tn):
    n, d = x2.shape
    n_out = w.shape[1]
    return pl.pallas_call(
        _proj_kernel,
        grid=(n // tm, n_out // tn),
        in_specs=[pl.BlockSpec((tm, d), lambda i, j: (i, 0)),
                  pl.BlockSpec((1, d), lambda i, j: (0, 0)),
                  pl.BlockSpec((d, tn), lambda i, j: (0, j))],
        out_specs=pl.BlockSpec((tm, tn), lambda i, j: (i, j)),
        out_shape=jax.ShapeDtypeStruct((n, n_out), F32),
        scratch_shapes=[pltpu.VMEM((tm, d), BF16)],
        compiler_params=pltpu.CompilerParams(
            dimension_semantics=("parallel", "arbitrary"), vmem_limit_bytes=VMEM_LIMIT),
        name="proj",
    )(x2, g, w)


def _sb_kernel(*refs, t_q, qb, n_past):
    if n_past:
        (q_ref, k_ref, v_ref, kc_ref, vc_ref, mc_ref,
         o_ref, ko_ref, vo_ref, q2_scr, k2_scr, v2_scr, later_scr, acc_scr) = refs
    else:
        (q_ref, k_ref, v_ref, mc_ref,
         o_ref, ko_ref, vo_ref, q2_scr, k2_scr, v2_scr, later_scr, acc_scr) = refs

    kp = k_ref[...]
    vp = v_ref[...]
    if t_q < KEY_BLOCK:
        pad = jnp.zeros((KEY_BLOCK - t_q, LANES), F32)
        kp = jnp.concatenate([kp, pad], axis=0)
        vp = jnp.concatenate([vp, pad], axis=0)
    q2_scr[...] = (q_ref[...] * (D_HEAD ** -0.5)).astype(BF16)

    kpt = kp.T
    vpt = vp.T
    for h in range(2):
        ko_ref[0, h] = kpt[h * D_HEAD:(h + 1) * D_HEAD, :t_q]
        vo_ref[0, h] = vpt[h * D_HEAD:(h + 1) * D_HEAD, :t_q]

    lane = lax.broadcasted_iota(jnp.int32, (KEY_BLOCK, LANES), 1)
    sub = lax.broadcasted_iota(jnp.int32, (LANES, KEY_BLOCK), 0)
    first_head = lane < D_HEAD
    first_head_rows = sub < D_HEAD

    def put_block(j, ktj, vj):
        k2_scr[j, :, :KEY_BLOCK] = jnp.where(first_head_rows, ktj, 0.0).astype(BF16)
        k2_scr[j, :, KEY_BLOCK:] = jnp.where(first_head_rows, 0.0, ktj).astype(BF16)
        v2_scr[j, :KEY_BLOCK] = jnp.where(first_head, vj, 0.0).astype(BF16)
        v2_scr[j, KEY_BLOCK:] = jnp.where(first_head, 0.0, vj).astype(BF16)

    for j in range(n_past):
        cols = slice(j * KEY_BLOCK, (j + 1) * KEY_BLOCK)
        put_block(j,
                  jnp.concatenate([kc_ref[0, 0, :, cols], kc_ref[0, 1, :, cols]], axis=0),
                  jnp.concatenate([vc_ref[0, 0, :, cols], vc_ref[0, 1, :, cols]], axis=0).T)
    for j in range(kp.shape[0] // KEY_BLOCK):
        cols = slice(j * KEY_BLOCK, (j + 1) * KEY_BLOCK)
        put_block(n_past + j, kpt[:, cols], vp[cols])

    row = lax.broadcasted_iota(jnp.int32, (qb, 2 * KEY_BLOCK), 0)
    col = lax.broadcasted_iota(jnp.int32, (qb, 2 * KEY_BLOCK), 1) % KEY_BLOCK
    earlier = col < row
    mc = mc_ref[...]
    n_sub = max(qb // KEY_BLOCK, 1)

    def q_block(i, carry):
        q0 = pl.multiple_of(i * qb, qb)
        later_scr[...] = jnp.zeros_like(later_scr)
        acc_scr[...] = jnp.zeros_like(acc_scr)

        def attend(r0, j, masked):
            m = qb - r0
            qi = q2_scr[pl.ds(q0 + r0, m), :]
            z = _dot(qi, k2_scr[j])
            sp = _softplus(z)
            lk = -sp
            if masked:
                lk = jnp.where(earlier[:m], lk, 0.0)
            lkb = lk.astype(BF16)
            sums = _dot(lkb, mc)
            later = later_scr[r0:, :]
            w = jnp.exp(z - sp + sums[:, :2 * KEY_BLOCK] - lkb.astype(F32) + later)
            if masked:
                w = jnp.where(earlier[:m], w, 0.0)
            later_scr[r0:, :] = later + sums[:, 2 * KEY_BLOCK:]
            acc_scr[r0:, :] = acc_scr[r0:, :] + _dot(w.astype(BF16), v2_scr[j])

        first = n_past + i * n_sub
        for d in reversed(range(n_sub)):
            attend(d * KEY_BLOCK, first + d, True)

        def more(c):
            return jnp.logical_and(c[0] >= 0, c[1] > EXP_ZERO)

        def older(c):
            j = c[0]
            qi = q2_scr[pl.ds(q0, qb), :]
            z = [_dot(qi, k2_scr[j - d]) for d in range(2)]
            sp = [_softplus(x) for x in z]
            lkb = [(-s).astype(BF16) for s in sp]
            sums = _dot(jnp.concatenate(lkb, axis=0), mc)
            later = [later_scr[...]]
            w = []
            for d in range(2):
                rows = slice(d * qb, (d + 1) * qb)
                w.append(jnp.exp(z[d] - sp[d] + sums[rows, :2 * KEY_BLOCK] - lkb[d].astype(F32)
                                 + later[d]).astype(BF16))
                later.append(later[d] + sums[rows, 2 * KEY_BLOCK:])
            later_scr[...] = later[2]
            v_two = v2_scr[pl.ds(j - 1, 2)].reshape(4 * KEY_BLOCK, LANES)
            acc_scr[...] = acc_scr[...] + _dot(jnp.concatenate([w[1], w[0]], axis=1), v_two)
            return j - 2, jnp.max(later[2])

        lax.while_loop(more, older, (first - 1, jnp.max(later_scr[...])))
        o_ref[pl.ds(q0, qb), :] = acc_scr[...]
        return carry

    lax.fori_loop(0, t_q // qb, q_block, 0)


def _cumsum_matrix():
    s = jnp.arange(KEY_BLOCK)
    incl = (s[:, None] >= s[None, :]).astype(F32)
    ones = jnp.ones((KEY_BLOCK, KEY_BLOCK), F32)
    z = jnp.zeros((KEY_BLOCK, KEY_BLOCK), F32)
    bd = lambda m: jnp.concatenate([jnp.concatenate([m, z], 1), jnp.concatenate([z, m], 1)], 0)
    return jnp.concatenate([bd(incl), bd(ones)], axis=1).astype(BF16)


def _sb(qkvg, k_past, v_past, batch, t_q):
    n = batch * t_q
    n_past = 0 if k_past is None else k_past.shape[3] // KEY_BLOCK
    qb = min(t_q, SB_Q_BLOCK)
    n_kb = n_past + pl.cdiv(t_q, KEY_BLOCK)
    assert n_past % 2 == 0 and (t_q <= KEY_BLOCK or qb % (2 * KEY_BLOCK) == 0), (n_past, t_q, qb)
    col = lambda off: pl.BlockSpec((t_q, LANES), lambda b, p: (b, off + p))
    in_specs = [col(0), col(N_PAIRS), col(2 * N_PAIRS)]
    args = [qkvg, qkvg, qkvg]
    if n_past:
        past = k_past.shape[3]
        cache = pl.BlockSpec((1, 2, D_HEAD, past), lambda b, p: (b, p, 0, 0))
        in_specs += [cache, cache]
        args += [k_past, v_past]
    in_specs.append(pl.BlockSpec((2 * KEY_BLOCK, 4 * KEY_BLOCK), lambda b, p: (0, 0)))
    args.append(_cumsum_matrix())
    kv_out = pl.BlockSpec((1, 2, D_HEAD, t_q), lambda b, p: (b, p, 0, 0))
    return pl.pallas_call(
        functools.partial(_sb_kernel, t_q=t_q, qb=qb, n_past=n_past),
        grid=(batch, N_PAIRS),
        in_specs=in_specs,
        out_specs=[pl.BlockSpec((t_q, LANES), lambda b, p: (b, p)), kv_out, kv_out],
        out_shape=[jax.ShapeDtypeStruct((n, D_MODEL), F32),
                   jax.ShapeDtypeStruct((batch, N_HEADS, D_HEAD, t_q), F32),
                   jax.ShapeDtypeStruct((batch, N_HEADS, D_HEAD, t_q), F32)],
        scratch_shapes=[pltpu.VMEM((t_q, LANES), BF16),
                        pltpu.VMEM((n_kb, LANES, 2 * KEY_BLOCK), BF16),
                        pltpu.VMEM((n_kb, 2 * KEY_BLOCK, LANES), BF16),
                        pltpu.VMEM((qb, 2 * KEY_BLOCK), F32),
                        pltpu.VMEM((qb, LANES), F32)],
        compiler_params=pltpu.CompilerParams(
            dimension_semantics=("parallel", "parallel"), vmem_limit_bytes=VMEM_LIMIT),
        name="sb",
    )(*args)


def _rwkv_kernel(pb_ref, sh_ref, s0_ref, mu_ref, w0_ref, w2_ref, a0_ref, a2_ref, kk_ref, ka_ref,
                 rk_ref, lg_ref, lb_ref, bd_ref, tri_ref,
                 og_ref, so_ref, s_scr, prev_scr, *, gb, rows):
    ch = RWKV_CHUNK
    c = pl.program_id(1)
    bd = bd_ref[...]
    tri = tri_ref[...]

    lane = lax.broadcasted_iota(jnp.int32, (ch, LANES), 1)
    row = lax.broadcasted_iota(jnp.int32, (ch, LANES), 0)
    first_head = lane < D_HEAD
    t_in = lane % D_HEAD
    strict = t_in < row
    incl = t_in <= row
    eye2 = jnp.where(t_in == row, 1.0, 0.0)
    same_head = (lax.broadcasted_iota(jnp.int32, (LANES, LANES), 0) // D_HEAD
                 == lax.broadcasted_iota(jnp.int32, (LANES, LANES), 1) // D_HEAD)

    def blockdiag(y):
        return jnp.concatenate([jnp.where(first_head, y, 0.0), jnp.where(first_head, 0.0, y)],
                               axis=0).astype(BF16)

    def pair_dot(x, y):
        return _dot(x.astype(BF16), blockdiag(y))

    def dot_nt(x, y):
        return lax.dot_general(x, y, (((1,), (1,)), ((), ())), preferred_element_type=F32)

    @pl.when(c == 0)
    def _():
        zero = jnp.zeros((D_HEAD, D_HEAD), F32)
        for b in range(gb):
            prev_scr[b] = sh_ref[b]
            for j in range(N_PAIRS):
                s_scr[b, j] = jnp.concatenate(
                    [jnp.concatenate([s0_ref[b, 2 * j], zero], axis=1),
                     jnp.concatenate([zero, s0_ref[b, 2 * j + 1]], axis=1)], axis=0)

    wide_row = lax.broadcasted_iota(jnp.int32, (ch, C_SHIFT), 0)
    live = lax.broadcasted_iota(jnp.int32, (ch, D_MODEL), 0) < rows
    for b in range(gb):
        p = pb_ref[b]
        if rows < ch:
            p = jnp.concatenate([p, jnp.zeros((ch - rows, C_SHIFT), F32)], axis=0)
        p_prev = jnp.where(wide_row == 0, prev_scr[b], pltpu.roll(p, 1, 0))
        prev_scr[b] = p[rows - 1:rows, :]
        pm = p + mu_ref[...] * (p_prev - p)
        r = pm[:, 0:D_MODEL]
        kb = pm[:, D_MODEL:2 * D_MODEL]
        vb = pm[:, 2 * D_MODEL:3 * D_MODEL]
        lora = pm[:, LORA_OFF:LORA_OFF + LANES]
        zb = pm[:, ZB_OFF:ZB_OFF + D_MODEL]
        w_raw = -_softplus(-(w0_ref[...] + _dot(jnp.tanh(lora).astype(BF16), w2_ref[...]))) - 0.5
        lw = -jnp.exp(w_raw)
        a = _sigmoid(a0_ref[...] + _dot(lora.astype(BF16), a2_ref[...]))
        kk = kb * kk_ref[...]
        norm = jnp.sqrt(_head_sum_wide(kk * kk, bd))
        kk = kk / jnp.maximum(norm, 1e-12)
        k2 = kb * (1.0 + (a - 1.0) * ka_ref[...])
        if rows < ch:
            r, kk, k2, vb, lw = (jnp.where(live, x, 0.0) for x in (r, kk, k2, vb, lw))
        be = kk * a
        bonus = _head_sum_wide(r * k2 * rk_ref[...], bd) * vb
        l1 = lw.astype(BF16)
        rem = lw - l1.astype(F32)
        l2 = rem.astype(BF16)
        l3 = (rem - l2.astype(F32)).astype(BF16)
        cs = _dot(tri, l1) + _dot(tri, l2) + _dot(tri, l3)

        pairs = range(N_PAIRS)
        tile = lambda arr, j: arr[:, j * LANES:(j + 1) * LANES]
        cs_end = [tile(cs, j)[ch - 1:ch, :] for j in pairs]
        g_inv = [jnp.exp(-tile(cs, j)) for j in pairs]
        x = [jnp.concatenate([tile(kk, j) * jnp.exp(tile(cs, j) - tile(lw, j)),
                              tile(r, j) * jnp.exp(tile(cs, j))], axis=0).astype(BF16) for j in pairs]
        gram = [dot_nt(x[j], jnp.concatenate([blockdiag(tile(be, j) * g_inv[j]),
                                              blockdiag(tile(k2, j) * g_inv[j])], axis=0)) for j in pairs]
        l_ab = [jnp.where(strict, gram[j][:ch, :LANES], 0.0) for j in pairs]
        m_ak = [jnp.where(strict, gram[j][:ch, LANES:], 0.0).astype(BF16) for j in pairs]
        m_r = [jnp.concatenate([jnp.where(incl, gram[j][ch:, :LANES], 0.0),
                                jnp.where(incl, gram[j][ch:, LANES:], 0.0)], axis=1).astype(BF16)
               for j in pairs]
        inv = [eye2 - l_ab[j] for j in pairs]
        pw = [pair_dot(l_ab[j], l_ab[j]) for j in pairs]
        n_prod = ch.bit_length() - 2
        for it in range(n_prod):
            inv = [inv[j] + pair_dot(inv[j], pw[j]) for j in pairs]
            if it < n_prod - 1:
                pw = [pair_dot(pw[j], pw[j]) for j in pairs]
        s_bd = [s_scr[b, j] for j in pairs]
        w1 = [dot_nt(x[j], s_bd[j].astype(BF16)) for j in pairs]
        v_bd = [blockdiag(tile(vb, j)) for j in pairs]
        z = [w1[j][:ch] + _dot(m_ak[j], v_bd[j]) for j in pairs]
        u = [-pair_dot(inv[j], z[j]) for j in pairs]
        outs = [w1[j][ch:] + _dot(m_r[j], jnp.concatenate([blockdiag(u[j]), v_bd[j]], axis=0))
                for j in pairs]
        for j in pairs:
            g_end = jnp.exp(cs_end[j] - tile(cs, j))
            uv_t = jnp.concatenate([u[j], tile(vb, j)], axis=0).T.astype(BF16)
            bk = jnp.concatenate([tile(be, j) * g_end, tile(k2, j) * g_end], axis=0).astype(BF16)
            s_scr[b, j] = s_bd[j] * jnp.exp(cs_end[j]) + jnp.where(same_head, _dot(uv_t, bk), 0.0)

        o = jnp.concatenate(outs, axis=1)
        mean = _head_sum_wide(o, bd) * (1.0 / D_HEAD)
        dlt = o - mean
        var = _head_sum_wide(dlt * dlt, bd) * (1.0 / D_HEAD)
        o = dlt * lax.rsqrt(var + LNX_EPS) * lg_ref[...] + lb_ref[...] + bonus
        og_ref[b] = (o * (zb * _sigmoid(zb)))[:rows]

    @pl.when(c == pl.num_programs(1) - 1)
    def _():
        for b in range(gb):
            for j in range(N_PAIRS):
                so_ref[b, 2 * j] = s_scr[b, j, :D_HEAD, :D_HEAD]
                so_ref[b, 2 * j + 1] = s_scr[b, j, D_HEAD:, D_HEAD:]


def _rwkv(pb, shift_prev, wkv_prev, prm, gb):
    batch, t, _ = pb.shape
    rows = min(t, RWKV_CHUNK)
    row = lambda v: v.reshape(1, -1)
    const = lambda shape: pl.BlockSpec(shape, lambda g, c: (0,) * len(shape))
    zeros = jnp.zeros((D_HEAD, D_MODEL), F32)
    w2p = jnp.concatenate([prm["w2"], zeros], axis=0).astype(BF16)
    a2p = jnp.concatenate([zeros, prm["a2"]], axis=0).astype(BF16)
    head = jnp.arange(LANES) // D_HEAD
    bd = (head[:, None] == head[None, :]).astype(BF16)
    tok = jnp.arange(RWKV_CHUNK)
    tri = (tok[:, None] >= tok[None, :]).astype(BF16)
    vec_spec = const((1, D_MODEL))
    return pl.pallas_call(
        functools.partial(_rwkv_kernel, gb=gb, rows=rows),
        grid=(batch // gb, t // rows),
        in_specs=[pl.BlockSpec((gb, rows, C_SHIFT), lambda g, c: (g, c, 0)),
                  pl.BlockSpec((gb, 1, C_SHIFT), lambda g, c: (g, 0, 0)),
                  pl.BlockSpec((gb, N_HEADS, D_HEAD, D_HEAD), lambda g, c: (g, 0, 0, 0)),
                  const((1, C_SHIFT)), vec_spec, const((LANES, D_MODEL)), vec_spec,
                  const((LANES, D_MODEL)), vec_spec, vec_spec, vec_spec, vec_spec, vec_spec,
                  const((LANES, LANES)), const((RWKV_CHUNK, RWKV_CHUNK))],
        out_specs=[pl.BlockSpec((gb, rows, D_MODEL), lambda g, c: (g, c, 0)),
                   pl.BlockSpec((gb, N_HEADS, D_HEAD, D_HEAD), lambda g, c: (g, 0, 0, 0))],
        out_shape=[jax.ShapeDtypeStruct((batch, t, D_MODEL), F32),
                   jax.ShapeDtypeStruct((batch, N_HEADS, D_HEAD, D_HEAD), F32)],
        scratch_shapes=[pltpu.VMEM((gb, N_PAIRS, LANES, LANES), F32),
                        pltpu.VMEM((gb, 1, C_SHIFT), F32)],
        compiler_params=pltpu.CompilerParams(
            dimension_semantics=("parallel", "arbitrary"), vmem_limit_bytes=VMEM_LIMIT),
        name="rwkv",
    )(pb, shift_prev, wkv_prev, row(prm["mu_shift"]), row(prm["w0"]), w2p, row(prm["a0"]), a2p,
      row(prm["k_k"]), row(prm["k_a"]), row(prm["r_k"]), row(prm["lnx_g"]), row(prm["lnx_b"]), bd, tri)


def _out_kernel(x_ref, za_ref, ga_ref, gb_ref, oa_ref, ob_ref, woa_ref, wob_ref, wout_ref, fg_ref,
                y_ref, *, final_norm):
    za = za_ref[...]
    y_a = _dot((oa_ref[...] * (za * _sigmoid(za))).astype(BF16), woa_ref[...])
    y_b = _dot(ob_ref[...].astype(BF16), wob_ref[...])
    merged = _sigmoid(ga_ref[...]) * y_a + _sigmoid(gb_ref[...]) * y_b
    y = x_ref[...] + _dot(merged.astype(BF16), wout_ref[...])
    if final_norm:
        ms = jnp.mean(y * y, axis=-1, keepdims=True)
        y = y * lax.rsqrt(ms + EPS) * fg_ref[...]
    y_ref[...] = y


def _out(x2, qkvg, o_a, og_b, w_o_a, w_o_b, w_out, final_g, tm, final_norm):
    n, d = x2.shape
    tok = lambda blk: pl.BlockSpec((tm, d), lambda i: (i, blk))
    wspec = pl.BlockSpec((d, d), lambda i: (0, 0))
    return pl.pallas_call(
        functools.partial(_out_kernel, final_norm=final_norm),
        grid=(n // tm,),
        in_specs=[tok(0), tok(3), tok(4), tok(5), tok(0), tok(0), wspec, wspec, wspec,
                  pl.BlockSpec((1, d), lambda i: (0, 0))],
        out_specs=tok(0),
        out_shape=jax.ShapeDtypeStruct((n, d), F32),
        compiler_params=pltpu.CompilerParams(
            dimension_semantics=("parallel",), vmem_limit_bytes=VMEM_LIMIT),
        name="out",
    )(x2, qkvg, qkvg, qkvg, o_a, og_b, w_o_a, w_o_b, w_out, final_g)


def _layer(x, shift_prev, k_past, v_past, wkv_prev, prm, final_g, final_norm):
    batch, t, d = x.shape
    n = batch * t
    x2 = x.reshape(n, d)
    g = prm["norm_g"].reshape(1, d)
    tm = min(n, 512)
    qkvg = _proj(x2, g, prm["w_in_a"], tm, 2 * D_MODEL)
    pb = _proj(x2, g, prm["w_in_b"], tm, C_SHIFT // 3).reshape(batch, t, C_SHIFT)
    dim_major = lambda c: None if c is None else jnp.swapaxes(c, -1, -2)
    o_a, k_new, v_new = _sb(qkvg, dim_major(k_past), dim_major(v_past), batch, t)
    og_b, wkv_new = _rwkv(pb, shift_prev, wkv_prev, prm, RWKV_GROUP)
    y = _out(x2, qkvg, o_a, og_b.reshape(n, d), prm["w_o_a"], prm["w_o_b"], prm["w_out"],
             final_g.reshape(1, d), min(n, 256), final_norm)
    return (y.reshape(batch, t, d), jnp.swapaxes(k_new, -1, -2), jnp.swapaxes(v_new, -1, -2),
            pb[:, t - 1:, :], wkv_new)


def kernel(x_prompt, x_sample, cache_sb_k, cache_sb_v, state_shift, state_wkv, norm_g, w_in, mu_shift,
           w0, w2, a0, a2, k_k, k_a, r_k, lnx_g, lnx_b, w_o_a, w_o_b, w_out, final_norm_g):
    depth = w_in.shape[0]
    batch = x_prompt.shape[0]
    xp, xs = x_prompt, x_sample
    outs_p, outs_s = [], []
    for l in range(depth):
        prm = dict(norm_g=norm_g[l],
                   w_in_a=w_in[l][:, :C_QKVG].astype(BF16), w_in_b=w_in[l][:, C_QKVG:].astype(BF16),
                   mu_shift=mu_shift[l], w0=w0[l], w2=w2[l], a0=a0[l], a2=a2[l], k_k=k_k[l], k_a=k_a[l],
                   r_k=r_k[l].reshape(-1), lnx_g=lnx_g[l], lnx_b=lnx_b[l],
                   w_o_a=w_o_a[l].astype(BF16), w_o_b=w_o_b[l].astype(BF16), w_out=w_out[l].astype(BF16))
        last = l == depth - 1
        zero_shift = jnp.zeros((batch, 1, C_SHIFT), F32)
        zero_wkv = jnp.zeros((batch, N_HEADS, D_HEAD, D_HEAD), F32)
        xp, *rest_p = _layer(xp, zero_shift, None, None, zero_wkv, prm, final_norm_g, last)
        xs, *rest_s = _layer(xs, state_shift[l], cache_sb_k[l], cache_sb_v[l], state_wkv[l],
                             prm, final_norm_g, last)
        outs_p.append(rest_p)
        outs_s.append(rest_s)
    stack = lambda outs, i: jnp.stack([o[i] for o in outs])
    return (xp, xs,
            stack(outs_p, 0), stack(outs_p, 1), stack(outs_p, 2), stack(outs_p, 3),
            stack(outs_s, 0), stack(outs_s, 1), stack(outs_s, 2), stack(outs_s, 3))
```

```python
import functools

import jax
import jax.numpy as jnp
from jax import lax
from jax.experimental import pallas as pl
from jax.experimental.pallas import tpu as pltpu

F32 = jnp.float32
BF16 = jnp.bfloat16

D_MODEL = 1024
N_HEADS = 16
D_HEAD = 64
LANES = 128
SUBLANES = 8
N_PAIRS = N_HEADS // 2
C_QKVG = 6 * D_MODEL
C_SHIFT = 3 * D_MODEL + 2 * D_HEAD + D_MODEL
LORA_OFF = 3 * D_MODEL
ZB_OFF = LORA_OFF + LANES
EPS = 1e-6
LNX_EPS = 64e-5
PROJ_ROWS = 1024
KEY_BLOCK = 128
SB_Q_BLOCK = 256
EXP_ZERO = -110.0
RWKV_GROUP = 4
RWKV_CHUNK = 64
VMEM_LIMIT = 56 * 1024 * 1024


def _softplus(x):
    return jnp.maximum(x, 0.0) + jnp.log(1.0 + jnp.exp(-jnp.abs(x)))


def _sigmoid(x):
    return 1.0 / (1.0 + jnp.exp(-x))


def _split_bf16(x):
    hi = x.astype(BF16)
    lo = (x - hi.astype(F32)).astype(BF16)
    return hi, lo


def _dot(a, b):
    return jnp.dot(a, b, preferred_element_type=F32)


def _head_sum(x, bd):
    hi, lo = _split_bf16(x)
    return _dot(hi, bd) + _dot(lo, bd)


def _head_sum_wide(x, bd):
    return jnp.concatenate(
        [_head_sum(x[:, j * LANES:(j + 1) * LANES], bd) for j in range(x.shape[1] // LANES)], axis=1)


def _proj_kernel(x_ref, g_ref, w_ref, o_ref, h_scr):
    @pl.when(pl.program_id(1) == 0)
    def _():
        x = x_ref[...]
        ms = jnp.mean(x * x, axis=-1, keepdims=True)
        h_scr[...] = (x * lax.rsqrt(ms + EPS) * g_ref[...]).astype(BF16)

    o_ref[...] = _dot(h_scr[...], w_ref[...])


def _proj(x2, g, w, tm, tn):
    n, d = x2.shape
    n_out = w.shape[1]
    return pl.pallas_call(
        _proj_kernel,
        grid=(n // tm, n_out // tn),
        in_specs=[pl.BlockSpec((tm, d), lambda i, j: (i, 0)),
                  pl.BlockSpec((1, d), lambda i, j: (0, 0)),
                  pl.BlockSpec((d, tn), lambda i, j: (0, j))],
        out_specs=pl.BlockSpec((tm, tn), lambda i, j: (i, j)),
        out_shape=jax.ShapeDtypeStruct((n, n_out), F32),
        scratch_shapes=[pltpu.VMEM((tm, d), BF16)],
        compiler_params=pltpu.CompilerParams(
            dimension_semantics=("parallel", "arbitrary"), vmem_limit_bytes=VMEM_LIMIT),
        name="proj",
    )(x2, g, w)


def _sb_kernel(*refs, t_q, qb, n_past, npair):
    if n_past:
        (q_ref, k_ref, v_ref, kc_ref, vc_ref, mc_ref,
         o_ref, ko_ref, vo_ref, q2_scr, k2_scr, v2_scr, later_scr, acc_scr) = refs
    else:
        (q_ref, k_ref, v_ref, mc_ref,
         o_ref, ko_ref, vo_ref, q2_scr, k2_scr, v2_scr, later_scr, acc_scr) = refs
    pairs = range(npair)
    tile = lambda g: slice(g * LANES, (g + 1) * LANES)

    lane = lax.broadcasted_iota(jnp.int32, (KEY_BLOCK, LANES), 1)
    sub = lax.broadcasted_iota(jnp.int32, (LANES, KEY_BLOCK), 0)
    first_head = lane < D_HEAD
    first_head_rows = sub < D_HEAD

    q2_scr[...] = (q_ref[...] * (D_HEAD ** -0.5)).astype(BF16)
    for g in pairs:
        kp = k_ref[:, tile(g)]
        vp = v_ref[:, tile(g)]
        if t_q < KEY_BLOCK:
            pad = jnp.zeros((KEY_BLOCK - t_q, LANES), F32)
            kp = jnp.concatenate([kp, pad], axis=0)
            vp = jnp.concatenate([vp, pad], axis=0)

        kpt = kp.T
        vpt = vp.T
        for h in range(2):
            ko_ref[0, 2 * g + h] = kpt[h * D_HEAD:(h + 1) * D_HEAD, :t_q]
            vo_ref[0, 2 * g + h] = vpt[h * D_HEAD:(h + 1) * D_HEAD, :t_q]

        def put_block(j, ktj, vj):
            k2_scr[g, j, :, :KEY_BLOCK] = jnp.where(first_head_rows, ktj, 0.0).astype(BF16)
            k2_scr[g, j, :, KEY_BLOCK:] = jnp.where(first_head_rows, 0.0, ktj).astype(BF16)
            v2_scr[g, j, :KEY_BLOCK] = jnp.where(first_head, vj, 0.0).astype(BF16)
            v2_scr[g, j, KEY_BLOCK:] = jnp.where(first_head, 0.0, vj).astype(BF16)

        for j in range(n_past):
            cols = slice(j * KEY_BLOCK, (j + 1) * KEY_BLOCK)
            put_block(j,
                      jnp.concatenate([kc_ref[0, 2 * g, :, cols], kc_ref[0, 2 * g + 1, :, cols]], axis=0),
                      jnp.concatenate([vc_ref[0, 2 * g, :, cols], vc_ref[0, 2 * g + 1, :, cols]], axis=0).T)
        for j in range(kp.shape[0] // KEY_BLOCK):
            cols = slice(j * KEY_BLOCK, (j + 1) * KEY_BLOCK)
            put_block(n_past + j, kpt[:, cols], vp[cols])

    row = lax.broadcasted_iota(jnp.int32, (qb, 2 * KEY_BLOCK), 0)
    col = lax.broadcasted_iota(jnp.int32, (qb, 2 * KEY_BLOCK), 1) % KEY_BLOCK
    earlier = col < row
    mc = mc_ref[...]
    n_sub = max(qb // KEY_BLOCK, 1)

    def q_block(i, carry):
        q0 = pl.multiple_of(i * qb, qb)

        def attend(r0, j):
            m = qb - r0
            z = [_dot(q2_scr[pl.ds(q0 + r0, m), tile(g)], k2_scr[g, j]) for g in pairs]
            sp = [_softplus(x) for x in z]
            lkb = [jnp.where(earlier[:m], -s, 0.0).astype(BF16) for s in sp]
            sums = [_dot(x, mc) for x in lkb]
            for g in pairs:
                later = later_scr[g, r0:, :]
                w = jnp.exp(z[g] - sp[g] + sums[g][:, :2 * KEY_BLOCK] - lkb[g].astype(F32) + later)
                w = jnp.where(earlier[:m], w, 0.0).astype(BF16)
                later_scr[g, r0:, :] = later + sums[g][:, 2 * KEY_BLOCK:]
                acc_scr[r0:, tile(g)] = acc_scr[r0:, tile(g)] + _dot(w, v2_scr[g, j])

        def attend_two(j):
            kb = KEY_BLOCK
            zero = jnp.zeros((kb, 2 * kb), F32)
            for g in pairs:
                z_new = _dot(q2_scr[pl.ds(q0 + kb, kb), tile(g)], k2_scr[g, j + 1])
                z_old = _dot(q2_scr[pl.ds(q0, qb), tile(g)], k2_scr[g, j])
                sp_new, sp_old = _softplus(z_new), _softplus(z_old)
                lkb_new = jnp.where(earlier[:kb], -sp_new, 0.0).astype(BF16)
                lkb_old = jnp.where(earlier, -sp_old, 0.0).astype(BF16)
                sums = _dot(jnp.concatenate([lkb_new, lkb_old], axis=0), mc)
                w_new = jnp.exp(z_new - sp_new + sums[:kb, :2 * kb] - lkb_new.astype(F32))
                w_new = jnp.where(earlier[:kb], w_new, 0.0).astype(BF16)
                later = jnp.concatenate([zero, sums[:kb, 2 * kb:]], axis=0)
                w_old = jnp.exp(z_old - sp_old + sums[kb:, :2 * kb] - lkb_old.astype(F32) + later)
                w_old = jnp.where(earlier, w_old, 0.0).astype(BF16)
                later_scr[g] = later + sums[kb:, 2 * kb:]
                acc_new = _dot(w_new, v2_scr[g, j + 1])
                acc_scr[:, tile(g)] = _dot(w_old, v2_scr[g, j]) + jnp.concatenate(
                    [jnp.zeros((kb, LANES), F32), acc_new], axis=0)

        first = n_past + i * n_sub
        if n_sub == 2:
            attend_two(first)
        else:
            later_scr[...] = jnp.zeros_like(later_scr)
            acc_scr[...] = jnp.zeros_like(acc_scr)
            for d in reversed(range(n_sub)):
                attend(d * KEY_BLOCK, first + d)

        def more(c):
            return jnp.logical_and(c[0] >= 0, c[1] > EXP_ZERO)

        def older(c):
            j = c[0]
            both = [(g, d) for g in pairs for d in range(2)]
            z = [_dot(q2_scr[pl.ds(q0, qb), tile(g)], k2_scr[g, j - d]) for g, d in both]
            sp = [_softplus(x) for x in z]
            lkb = [(-s).astype(BF16) for s in sp]
            sums = [_dot(jnp.concatenate(lkb[2 * g:2 * g + 2], axis=0), mc) for g in pairs]
            top = []
            for g in pairs:
                later = [later_scr[g]]
                w = []
                for d in range(2):
                    rows = slice(d * qb, (d + 1) * qb)
                    w.append(jnp.exp(z[2 * g + d] - sp[2 * g + d] + sums[g][rows, :2 * KEY_BLOCK]
                                     - lkb[2 * g + d].astype(F32) + later[d]).astype(BF16))
                    later.append(later[d] + sums[g][rows, 2 * KEY_BLOCK:])
                later_scr[g] = later[2]
                top.append(jnp.max(later[2]))
                v_two = v2_scr[g, pl.ds(j - 1, 2)].reshape(4 * KEY_BLOCK, LANES)
                acc_scr[:, tile(g)] = acc_scr[:, tile(g)] + _dot(jnp.concatenate([w[1], w[0]], axis=1), v_two)
            return j - 2, functools.reduce(jnp.maximum, top)

        lax.while_loop(more, older, (first - 1, jnp.max(later_scr[...])))
        o_ref[pl.ds(q0, qb), :] = acc_scr[...]
        return carry

    lax.fori_loop(0, t_q // qb, q_block, 0)


def _cumsum_matrix():
    s = jnp.arange(KEY_BLOCK)
    incl = (s[:, None] >= s[None, :]).astype(F32)
    ones = jnp.ones((KEY_BLOCK, KEY_BLOCK), F32)
    z = jnp.zeros((KEY_BLOCK, KEY_BLOCK), F32)
    bd = lambda m: jnp.concatenate([jnp.concatenate([m, z], 1), jnp.concatenate([z, m], 1)], 0)
    return jnp.concatenate([bd(incl), bd(ones)], axis=1).astype(BF16)


def _sb(qkvg, k_past, v_past, batch, t_q):
    n = batch * t_q
    n_past = 0 if k_past is None else k_past.shape[3] // KEY_BLOCK
    qb = min(t_q, SB_Q_BLOCK)
    n_kb = n_past + pl.cdiv(t_q, KEY_BLOCK)
    assert n_past % 2 == 0 and (t_q <= KEY_BLOCK or qb % (2 * KEY_BLOCK) == 0), (n_past, t_q, qb)
    npair = N_PAIRS if t_q <= KEY_BLOCK else 1
    steps = N_PAIRS // npair
    width = npair * LANES
    col = lambda section: pl.BlockSpec((t_q, width), lambda b, p: (b, section * steps + p))
    in_specs = [col(0), col(1), col(2)]
    args = [qkvg, qkvg, qkvg]
    if n_past:
        past = k_past.shape[3]
        cache = pl.BlockSpec((1, 2 * npair, D_HEAD, past), lambda b, p: (b, p, 0, 0))
        in_specs += [cache, cache]
        args += [k_past, v_past]
    in_specs.append(pl.BlockSpec((2 * KEY_BLOCK, 4 * KEY_BLOCK), lambda b, p: (0, 0)))
    args.append(_cumsum_matrix())
    kv_out = pl.BlockSpec((1, 2 * npair, D_HEAD, t_q), lambda b, p: (b, p, 0, 0))
    return pl.pallas_call(
        functools.partial(_sb_kernel, t_q=t_q, qb=qb, n_past=n_past, npair=npair),
        grid=(batch, steps),
        in_specs=in_specs,
        out_specs=[pl.BlockSpec((t_q, width), lambda b, p: (b, p)), kv_out, kv_out],
        out_shape=[jax.ShapeDtypeStruct((n, D_MODEL), F32),
                   jax.ShapeDtypeStruct((batch, N_HEADS, D_HEAD, t_q), F32),
                   jax.ShapeDtypeStruct((batch, N_HEADS, D_HEAD, t_q), F32)],
        scratch_shapes=[pltpu.VMEM((t_q, width), BF16),
                        pltpu.VMEM((npair, n_kb, LANES, 2 * KEY_BLOCK), BF16),
                        pltpu.VMEM((npair, n_kb, 2 * KEY_BLOCK, LANES), BF16),
                        pltpu.VMEM((npair, qb, 2 * KEY_BLOCK), F32),
                        pltpu.VMEM((qb, width), F32)],
        compiler_params=pltpu.CompilerParams(
            dimension_semantics=("parallel", "parallel"), vmem_limit_bytes=VMEM_LIMIT),
        name="sb",
    )(*args)


def _rwkv_kernel(pb_ref, sh_ref, s0_ref, mu_ref, w0_ref, w2_ref, a0_ref, a2_ref, kk_ref, ka_ref,
                 rk_ref, lg_ref, lb_ref, bd_ref, tri_ref,
                 og_ref, so_ref, s_scr, prev_scr, *, gb, rows):
    ch = RWKV_CHUNK
    c = pl.program_id(1)
    bd = bd_ref[...]
    tri = tri_ref[...]

    lane = lax.broadcasted_iota(jnp.int32, (ch, LANES), 1)
    row = lax.broadcasted_iota(jnp.int32, (ch, LANES), 0)
    first_head = lane < D_HEAD
    t_in = lane % D_HEAD
    strict = t_in < row
    incl = t_in <= row
    eye2 = jnp.where(t_in == row, 1.0, 0.0)
    same_head = (lax.broadcasted_iota(jnp.int32, (LANES, LANES), 0) // D_HEAD
                 == lax.broadcasted_iota(jnp.int32, (LANES, LANES), 1) // D_HEAD)

    def blockdiag(y):
        return jnp.concatenate([jnp.where(first_head, y, 0.0), jnp.where(first_head, 0.0, y)],
                               axis=0).astype(BF16)

    def pair_dot(x, y):
        return _dot(x.astype(BF16), blockdiag(y))

    def dot_nt(x, y):
        return lax.dot_general(x, y, (((1,), (1,)), ((), ())), preferred_element_type=F32)

    @pl.when(c == 0)
    def _():
        zero = jnp.zeros((D_HEAD, D_HEAD), F32)
        for b in range(gb):
            prev_scr[b] = sh_ref[b]
            for j in range(N_PAIRS):
                s_scr[b, j] = jnp.concatenate(
                    [jnp.concatenate([s0_ref[b, 2 * j], zero], axis=1),
                     jnp.concatenate([zero, s0_ref[b, 2 * j + 1]], axis=1)], axis=0)

    wide_row = lax.broadcasted_iota(jnp.int32, (ch, C_SHIFT), 0)
    live = lax.broadcasted_iota(jnp.int32, (ch, D_MODEL), 0) < rows
    tok = []
    for b in range(gb):
        p = pb_ref[b]
        if rows < ch:
            p = jnp.concatenate([p, jnp.zeros((ch - rows, C_SHIFT), F32)], axis=0)
        p_prev = jnp.where(wide_row == 0, prev_scr[b], pltpu.roll(p, 1, 0))
        prev_scr[b] = p[rows - 1:rows, :]
        pm = p + mu_ref[...] * (p_prev - p)
        r = pm[:, 0:D_MODEL]
        kb = pm[:, D_MODEL:2 * D_MODEL]
        vb = pm[:, 2 * D_MODEL:3 * D_MODEL]
        lora = pm[:, LORA_OFF:LORA_OFF + LANES]
        zb = pm[:, ZB_OFF:ZB_OFF + D_MODEL]
        w_raw = -_softplus(-(w0_ref[...] + _dot(jnp.tanh(lora).astype(BF16), w2_ref[...]))) - 0.5
        lw = -jnp.exp(w_raw)
        a = _sigmoid(a0_ref[...] + _dot(lora.astype(BF16), a2_ref[...]))
        kk = kb * kk_ref[...]
        norm = jnp.sqrt(_head_sum_wide(kk * kk, bd))
        kk = kk / jnp.maximum(norm, 1e-12)
        k2 = kb * (1.0 + (a - 1.0) * ka_ref[...])
        if rows < ch:
            r, kk, k2, vb, lw = (jnp.where(live, x, 0.0) for x in (r, kk, k2, vb, lw))
        l1 = lw.astype(BF16)
        rem = lw - l1.astype(F32)
        l2 = rem.astype(BF16)
        l3 = (rem - l2.astype(F32)).astype(BF16)
        tok.append(dict(r=r, kk=kk, k2=k2, vb=vb, lw=lw, be=kk * a, zb=zb,
                        bonus=_head_sum_wide(r * k2 * rk_ref[...], bd) * vb,
                        cs=_dot(tri, l1) + _dot(tri, l2) + _dot(tri, l3)))

    chains = [(b, j) for b in range(gb) for j in range(N_PAIRS)]
    pairs = range(len(chains))
    tile = lambda name, q: tok[chains[q][0]][name][:, chains[q][1] * LANES:(chains[q][1] + 1) * LANES]
    cs_end = [tile("cs", q)[ch - 1:ch, :] for q in pairs]
    g_inv = [jnp.exp(-tile("cs", q)) for q in pairs]
    x = [jnp.concatenate([tile("kk", q) * jnp.exp(tile("cs", q) - tile("lw", q)),
                          tile("r", q) * jnp.exp(tile("cs", q))], axis=0).astype(BF16) for q in pairs]
    gram = [dot_nt(x[q], jnp.concatenate([blockdiag(tile("be", q) * g_inv[q]),
                                          blockdiag(tile("k2", q) * g_inv[q])], axis=0)) for q in pairs]
    l_ab = [jnp.where(strict, gram[q][:ch, :LANES], 0.0) for q in pairs]
    m_ak = [jnp.where(strict, gram[q][:ch, LANES:], 0.0).astype(BF16) for q in pairs]
    m_r = [jnp.concatenate([jnp.where(incl, gram[q][ch:, :LANES], 0.0),
                            jnp.where(incl, gram[q][ch:, LANES:], 0.0)], axis=1).astype(BF16)
           for q in pairs]
    inv = [eye2 - l_ab[q] for q in pairs]
    pw = [pair_dot(l_ab[q], l_ab[q]) for q in pairs]
    n_prod = ch.bit_length() - 2
    for it in range(n_prod):
        if it < n_prod - 1:
            both = [pair_dot(jnp.concatenate([inv[q], pw[q]], axis=0), pw[q]) for q in pairs]
            inv = [inv[q] + both[q][:ch] for q in pairs]
            pw = [both[q][ch:] for q in pairs]
        else:
            inv = [inv[q] + pair_dot(inv[q], pw[q]) for q in pairs]
    s_bd = [s_scr[chains[q]] for q in pairs]
    w1 = [dot_nt(x[q], s_bd[q].astype(BF16)) for q in pairs]
    v_bd = [blockdiag(tile("vb", q)) for q in pairs]
    z = [w1[q][:ch] + _dot(m_ak[q], v_bd[q]) for q in pairs]
    u = [-pair_dot(inv[q], z[q]) for q in pairs]
    outs = [w1[q][ch:] + _dot(m_r[q], jnp.concatenate([blockdiag(u[q]), v_bd[q]], axis=0))
            for q in pairs]
    for q in pairs:
        g_end = jnp.exp(cs_end[q] - tile("cs", q))
        uv_t = jnp.concatenate([u[q], tile("vb", q)], axis=0).T.astype(BF16)
        bk = jnp.concatenate([tile("be", q) * g_end, tile("k2", q) * g_end], axis=0).astype(BF16)
        s_scr[chains[q]] = s_bd[q] * jnp.exp(cs_end[q]) + jnp.where(same_head, _dot(uv_t, bk), 0.0)

    for b in range(gb):
        o = jnp.concatenate(outs[b * N_PAIRS:(b + 1) * N_PAIRS], axis=1)
        mean = _head_sum_wide(o, bd) * (1.0 / D_HEAD)
        dlt = o - mean
        var = _head_sum_wide(dlt * dlt, bd) * (1.0 / D_HEAD)
        o = dlt * lax.rsqrt(var + LNX_EPS) * lg_ref[...] + lb_ref[...] + tok[b]["bonus"]
        zb = tok[b]["zb"]
        og_ref[b] = (o * (zb * _sigmoid(zb)))[:rows]

    @pl.when(c == pl.num_programs(1) - 1)
    def _():
        for b in range(gb):
            for j in range(N_PAIRS):
                so_ref[b, 2 * j] = s_scr[b, j, :D_HEAD, :D_HEAD]
                so_ref[b, 2 * j + 1] = s_scr[b, j, D_HEAD:, D_HEAD:]


def _rwkv(pb, shift_prev, wkv_prev, prm, gb):
    batch, t, _ = pb.shape
    rows = min(t, RWKV_CHUNK)
    row = lambda v: v.reshape(1, -1)
    const = lambda shape: pl.BlockSpec(shape, lambda g, c: (0,) * len(shape))
    zeros = jnp.zeros((D_HEAD, D_MODEL), F32)
    w2p = jnp.concatenate([prm["w2"], zeros], axis=0).astype(BF16)
    a2p = jnp.concatenate([zeros, prm["a2"]], axis=0).astype(BF16)
    head = jnp.arange(LANES) // D_HEAD
    bd = (head[:, None] == head[None, :]).astype(BF16)
    tok = jnp.arange(RWKV_CHUNK)
    tri = (tok[:, None] >= tok[None, :]).astype(BF16)
    vec_spec = const((1, D_MODEL))
    return pl.pallas_call(
        functools.partial(_rwkv_kernel, gb=gb, rows=rows),
        grid=(batch // gb, t // rows),
        in_specs=[pl.BlockSpec((gb, rows, C_SHIFT), lambda g, c: (g, c, 0)),
                  pl.BlockSpec((gb, 1, C_SHIFT), lambda g, c: (g, 0, 0)),
                  pl.BlockSpec((gb, N_HEADS, D_HEAD, D_HEAD), lambda g, c: (g, 0, 0, 0)),
                  const((1, C_SHIFT)), vec_spec, const((LANES, D_MODEL)), vec_spec,
                  const((LANES, D_MODEL)), vec_spec, vec_spec, vec_spec, vec_spec, vec_spec,
                  const((LANES, LANES)), const((RWKV_CHUNK, RWKV_CHUNK))],
        out_specs=[pl.BlockSpec((gb, rows, D_MODEL), lambda g, c: (g, c, 0)),
                   pl.BlockSpec((gb, N_HEADS, D_HEAD, D_HEAD), lambda g, c: (g, 0, 0, 0))],
        out_shape=[jax.ShapeDtypeStruct((batch, t, D_MODEL), F32),
                   jax.ShapeDtypeStruct((batch, N_HEADS, D_HEAD, D_HEAD), F32)],
        scratch_shapes=[pltpu.VMEM((gb, N_PAIRS, LANES, LANES), F32),
                        pltpu.VMEM((gb, 1, C_SHIFT), F32)],
        compiler_params=pltpu.CompilerParams(
            dimension_semantics=("parallel", "arbitrary"), vmem_limit_bytes=VMEM_LIMIT),
        name="rwkv",
    )(pb, shift_prev, wkv_prev, row(prm["mu_shift"]), row(prm["w0"]), w2p, row(prm["a0"]), a2p,
      row(prm["k_k"]), row(prm["k_a"]), row(prm["r_k"]), row(prm["lnx_g"]), row(prm["lnx_b"]), bd, tri)


def _out_kernel(x_ref, za_ref, ga_ref, gb_ref, oa_ref, ob_ref, woa_ref, wob_ref, wout_ref, fg_ref,
                y_ref, *, final_norm):
    za = za_ref[...]
    y_a = _dot((oa_ref[...] * (za * _sigmoid(za))).astype(BF16), woa_ref[...])
    y_b = _dot(ob_ref[...].astype(BF16), wob_ref[...])
    merged = _sigmoid(ga_ref[...]) * y_a + _sigmoid(gb_ref[...]) * y_b
    y = x_ref[...] + _dot(merged.astype(BF16), wout_ref[...])
    if final_norm:
        ms = jnp.mean(y * y, axis=-1, keepdims=True)
        y = y * lax.rsqrt(ms + EPS) * fg_ref[...]
    y_ref[...] = y


def _out(x2, qkvg, o_a, og_b, w_o_a, w_o_b, w_out, final_g, tm, final_norm):
    n, d = x2.shape
    tok = lambda blk: pl.BlockSpec((tm, d), lambda i: (i, blk))
    wspec = pl.BlockSpec((d, d), lambda i: (0, 0))
    return pl.pallas_call(
        functools.partial(_out_kernel, final_norm=final_norm),
        grid=(n // tm,),
        in_specs=[tok(0), tok(3), tok(4), tok(5), tok(0), tok(0), wspec, wspec, wspec,
                  pl.BlockSpec((1, d), lambda i: (0, 0))],
        out_specs=tok(0),
        out_shape=jax.ShapeDtypeStruct((n, d), F32),
        compiler_params=pltpu.CompilerParams(
            dimension_semantics=("parallel",), vmem_limit_bytes=VMEM_LIMIT),
        name="out",
    )(x2, qkvg, qkvg, qkvg, o_a, og_b, w_o_a, w_o_b, w_out, final_g)


def _layer(x, shift_prev, k_past, v_past, wkv_prev, prm, final_g, final_norm):
    batch, t, d = x.shape
    n = batch * t
    x2 = x.reshape(n, d)
    g = prm["norm_g"].reshape(1, d)
    tm = min(n, PROJ_ROWS)
    qkvg = _proj(x2, g, prm["w_in_a"], tm, 2 * D_MODEL)
    pb = _proj(x2, g, prm["w_in_b"], tm, C_SHIFT // 3).reshape(batch, t, C_SHIFT)
    dim_major = lambda c: None if c is None else jnp.swapaxes(c, -1, -2)
    o_a, k_new, v_new = _sb(qkvg, dim_major(k_past), dim_major(v_past), batch, t)
    og_b, wkv_new = _rwkv(pb, shift_prev, wkv_prev, prm, RWKV_GROUP)
    y = _out(x2, qkvg, o_a, og_b.reshape(n, d), prm["w_o_a"], prm["w_o_b"], prm["w_out"],
             final_g.reshape(1, d), min(n, 256), final_norm)
    return (y.reshape(batch, t, d), jnp.swapaxes(k_new, -1, -2), jnp.swapaxes(v_new, -1, -2),
            pb[:, t - 1:, :], wkv_new)


def kernel(x_prompt, x_sample, cache_sb_k, cache_sb_v, state_shift, state_wkv, norm_g, w_in, mu_shift,
           w0, w2, a0, a2, k_k, k_a, r_k, lnx_g, lnx_b, w_o_a, w_o_b, w_out, final_norm_g):
    depth = w_in.shape[0]
    batch = x_prompt.shape[0]
    xp, xs = x_prompt, x_sample
    outs_p, outs_s = [], []
    for l in range(depth):
        prm = dict(norm_g=norm_g[l],
                   w_in_a=w_in[l][:, :C_QKVG].astype(BF16), w_in_b=w_in[l][:, C_QKVG:].astype(BF16),
                   mu_shift=mu_shift[l], w0=w0[l], w2=w2[l], a0=a0[l], a2=a2[l], k_k=k_k[l], k_a=k_a[l],
                   r_k=r_k[l].reshape(-1), lnx_g=lnx_g[l], lnx_b=lnx_b[l],
                   w_o_a=w_o_a[l].astype(BF16), w_o_b=w_o_b[l].astype(BF16), w_out=w_out[l].astype(BF16))
        last = l == depth - 1
        zero_shift = jnp.zeros((batch, 1, C_SHIFT), F32)
        zero_wkv = jnp.zeros((batch, N_HEADS, D_HEAD, D_HEAD), F32)
        xp, *rest_p = _layer(xp, zero_shift, None, None, zero_wkv, prm, final_norm_g, last)
        xs, *rest_s = _layer(xs, state_shift[l], cache_sb_k[l], cache_sb_v[l], state_wkv[l],
                             prm, final_norm_g, last)
        outs_p.append(rest_p)
        outs_s.append(rest_s)
    stack = lambda outs, i: jnp.stack([o[i] for o in outs])
    return (xp, xs,
            stack(outs_p, 0), stack(outs_p, 1), stack(outs_p, 2), stack(outs_p, 3),
            stack(outs_s, 0), stack(outs_s, 1), stack(outs_s, 2), stack(outs_s, 3))
```

```python
import functools

import jax
import jax.numpy as jnp
from jax import lax
from jax.experimental import pallas as pl
from jax.experimental.pallas import tpu as pltpu

F32 = jnp.float32
BF16 = jnp.bfloat16

D_MODEL = 1024
N_HEADS = 16
D_HEAD = 64
LANES = 128
SUBLANES = 8
N_PAIRS = N_HEADS // 2
C_QKVG = 6 * D_MODEL
C_SHIFT = 3 * D_MODEL + 2 * D_HEAD + D_MODEL
LORA_OFF = 3 * D_MODEL
ZB_OFF = LORA_OFF + LANES
EPS = 1e-6
LNX_EPS = 64e-5
PROJ_ROWS = 1024
OUT_ROWS = 512
KEY_BLOCK = 128
SB_Q_BLOCK = 256
EXP_ZERO = -110.0
RWKV_GROUP = 4
RWKV_CHUNK = 64
VMEM_LIMIT = 56 * 1024 * 1024


def _softplus(x):
    return jnp.maximum(x, 0.0) + jnp.log(1.0 + jnp.exp(-jnp.abs(x)))


def _sigmoid(x):
    return 1.0 / (1.0 + jnp.exp(-x))


def _dot(a, b):
    return jnp.dot(a, b, preferred_element_type=F32)


def _head_sum(x, bd):
    return _dot(x.astype(BF16), bd)


def _head_sum_wide(x, bd):
    return jnp.concatenate(
        [_head_sum(x[:, j * LANES:(j + 1) * LANES], bd) for j in range(x.shape[1] // LANES)], axis=1)


def _proj_kernel(x_ref, g_ref, w_ref, o_ref, h_scr):
    @pl.when(pl.program_id(1) == 0)
    def _():
        x = x_ref[...]
        ms = jnp.mean(x * x, axis=-1, keepdims=True)
        h_scr[...] = (x * lax.rsqrt(ms + EPS) * g_ref[...]).astype(BF16)

    o_ref[...] = _dot(h_scr[...], w_ref[...])


def _proj(x2, g, w, tm, tn):
    n, d = x2.shape
    n_out = w.shape[1]
    return pl.pallas_call(
        _proj_kernel,
        grid=(n // tm, n_out // tn),
        in_specs=[pl.BlockSpec((tm, d), lambda i, j: (i, 0)),
                  pl.BlockSpec((1, d), lambda i, j: (0, 0)),
                  pl.BlockSpec((d, tn), lambda i, j: (0, j))],
        out_specs=pl.BlockSpec((tm, tn), lambda i, j: (i, j)),
        out_shape=jax.ShapeDtypeStruct((n, n_out), F32),
        scratch_shapes=[pltpu.VMEM((tm, d), BF16)],
        compiler_params=pltpu.CompilerParams(
            dimension_semantics=("parallel", "arbitrary"), vmem_limit_bytes=VMEM_LIMIT),
        name="proj",
    )(x2, g, w)


def _sb_kernel(*refs, t_q, qb, n_past, npair):
    if n_past:
        (q_ref, k_ref, v_ref, kc_ref, vc_ref, mc_ref,
         o_ref, ko_ref, vo_ref, q2_scr, k2_scr, v2_scr, later_scr, acc_scr) = refs
    else:
        (q_ref, k_ref, v_ref, mc_ref,
         o_ref, ko_ref, vo_ref, q2_scr, k2_scr, v2_scr, later_scr, acc_scr) = refs
    pairs = range(npair)
    tile = lambda g: slice(g * LANES, (g + 1) * LANES)

    lane = lax.broadcasted_iota(jnp.int32, (KEY_BLOCK, LANES), 1)
    sub = lax.broadcasted_iota(jnp.int32, (LANES, KEY_BLOCK), 0)
    first_head = lane < D_HEAD
    first_head_rows = sub < D_HEAD

    q2_scr[...] = (q_ref[...] * (D_HEAD ** -0.5)).astype(BF16)
    for g in pairs:
        kp = k_ref[:, tile(g)]
        vp = v_ref[:, tile(g)]
        if t_q < KEY_BLOCK:
            pad = jnp.zeros((KEY_BLOCK - t_q, LANES), F32)
            kp = jnp.concatenate([kp, pad], axis=0)
            vp = jnp.concatenate([vp, pad], axis=0)

        kpt = kp.T
        vpt = vp.T
        for h in range(2):
            ko_ref[0, 2 * g + h] = kpt[h * D_HEAD:(h + 1) * D_HEAD, :t_q]
            vo_ref[0, 2 * g + h] = vpt[h * D_HEAD:(h + 1) * D_HEAD, :t_q]

        def put_block(j, ktj, vj):
            k2_scr[g, j, :, :KEY_BLOCK] = jnp.where(first_head_rows, ktj, 0.0).astype(BF16)
            k2_scr[g, j, :, KEY_BLOCK:] = jnp.where(first_head_rows, 0.0, ktj).astype(BF16)
            v2_scr[g, j, :KEY_BLOCK] = jnp.where(first_head, vj, 0.0).astype(BF16)
            v2_scr[g, j, KEY_BLOCK:] = jnp.where(first_head, 0.0, vj).astype(BF16)

        for j in range(n_past):
            cols = slice(j * KEY_BLOCK, (j + 1) * KEY_BLOCK)
            put_block(j,
                      jnp.concatenate([kc_ref[0, 2 * g, :, cols], kc_ref[0, 2 * g + 1, :, cols]], axis=0),
                      jnp.concatenate([vc_ref[0, 2 * g, :, cols], vc_ref[0, 2 * g + 1, :, cols]], axis=0).T)
        for j in range(kp.shape[0] // KEY_BLOCK):
            cols = slice(j * KEY_BLOCK, (j + 1) * KEY_BLOCK)
            put_block(n_past + j, kpt[:, cols], vp[cols])

    row = lax.broadcasted_iota(jnp.int32, (qb, 2 * KEY_BLOCK), 0)
    col = lax.broadcasted_iota(jnp.int32, (qb, 2 * KEY_BLOCK), 1) % KEY_BLOCK
    earlier = col < row
    mc = mc_ref[...]

    def suffix_sums(lkb):
        return _dot(lkb, mc)
    n_sub = max(qb // KEY_BLOCK, 1)

    def q_block(i, carry):
        q0 = pl.multiple_of(i * qb, qb)

        def attend(r0, j):
            m = qb - r0
            z = [_dot(q2_scr[pl.ds(q0 + r0, m), tile(g)], k2_scr[g, j]) for g in pairs]
            sp = [_softplus(x) for x in z]
            lkb = [jnp.where(earlier[:m], -s, 0.0).astype(BF16) for s in sp]
            sums = [suffix_sums(x) for x in lkb]
            for g in pairs:
                later = later_scr[g, r0:, :]
                w = jnp.exp(z[g] - sp[g] + sums[g][:, :2 * KEY_BLOCK] - lkb[g].astype(F32) + later)
                w = jnp.where(earlier[:m], w, 0.0).astype(BF16)
                later_scr[g, r0:, :] = later + sums[g][:, 2 * KEY_BLOCK:]
                acc_scr[r0:, tile(g)] = acc_scr[r0:, tile(g)] + _dot(w, v2_scr[g, j])

        def attend_two(j):
            kb = KEY_BLOCK
            zero = jnp.zeros((kb, 2 * kb), F32)
            for g in pairs:
                z_new = _dot(q2_scr[pl.ds(q0 + kb, kb), tile(g)], k2_scr[g, j + 1])
                z_old = _dot(q2_scr[pl.ds(q0, qb), tile(g)], k2_scr[g, j])
                sp_new, sp_old = _softplus(z_new), _softplus(z_old)
                lkb_new = jnp.where(earlier[:kb], -sp_new, 0.0).astype(BF16)
                lkb_old = jnp.where(earlier, -sp_old, 0.0).astype(BF16)
                sums = suffix_sums(jnp.concatenate([lkb_new, lkb_old], axis=0))
                w_new = jnp.exp(z_new - sp_new + sums[:kb, :2 * kb] - lkb_new.astype(F32))
                w_new = jnp.where(earlier[:kb], w_new, 0.0).astype(BF16)
                later = jnp.concatenate([zero, sums[:kb, 2 * kb:]], axis=0)
                w_old = jnp.exp(z_old - sp_old + sums[kb:, :2 * kb] - lkb_old.astype(F32) + later)
                w_old = jnp.where(earlier, w_old, 0.0).astype(BF16)
                later_scr[g] = later + sums[kb:, 2 * kb:]
                acc_new = _dot(w_new, v2_scr[g, j + 1])
                acc_scr[:, tile(g)] = _dot(w_old, v2_scr[g, j]) + jnp.concatenate(
                    [jnp.zeros((kb, LANES), F32), acc_new], axis=0)

        first = n_past + i * n_sub
        if n_sub == 2:
            attend_two(first)
        else:
            later_scr[...] = jnp.zeros_like(later_scr)
            acc_scr[...] = jnp.zeros_like(acc_scr)
            for d in reversed(range(n_sub)):
                attend(d * KEY_BLOCK, first + d)

        def more(c):
            return jnp.logical_and(c[0] >= 0, c[1] > EXP_ZERO)

        def older(c):
            j = c[0]
            both = [(g, d) for g in pairs for d in range(2)]
            z = [_dot(q2_scr[pl.ds(q0, qb), tile(g)], k2_scr[g, j - d]) for g, d in both]
            sp = [_softplus(x) for x in z]
            lkb = [(-s).astype(BF16) for s in sp]
            sums = [suffix_sums(jnp.concatenate(lkb[2 * g:2 * g + 2], axis=0)) for g in pairs]
            top = []
            for g in pairs:
                later = [later_scr[g]]
                w = []
                for d in range(2):
                    rows = slice(d * qb, (d + 1) * qb)
                    w.append(jnp.exp(z[2 * g + d] - sp[2 * g + d] + sums[g][rows, :2 * KEY_BLOCK]
                                     - lkb[2 * g + d].astype(F32) + later[d]).astype(BF16))
                    later.append(later[d] + sums[g][rows, 2 * KEY_BLOCK:])
                later_scr[g] = later[2]
                top.append(jnp.max(later[2]))
                v_two = v2_scr[g, pl.ds(j - 1, 2)].reshape(4 * KEY_BLOCK, LANES)
                acc_scr[:, tile(g)] = acc_scr[:, tile(g)] + _dot(jnp.concatenate([w[1], w[0]], axis=1), v_two)
            return j - 2, functools.reduce(jnp.maximum, top)

        lax.while_loop(more, older, (first - 1, jnp.max(later_scr[...])))
        o_ref[pl.ds(q0, qb), :] = acc_scr[...]
        return carry

    lax.fori_loop(0, t_q // qb, q_block, 0)


def _cumsum_matrix():
    s = jnp.arange(KEY_BLOCK)
    incl = (s[:, None] >= s[None, :]).astype(F32)
    ones = jnp.ones((KEY_BLOCK, KEY_BLOCK), F32)
    z = jnp.zeros((KEY_BLOCK, KEY_BLOCK), F32)
    bd = lambda m: jnp.concatenate([jnp.concatenate([m, z], 1), jnp.concatenate([z, m], 1)], 0)
    return jnp.concatenate([bd(incl), bd(ones)], axis=1).astype(BF16)


def _sb(qkvg, k_past, v_past, batch, t_q):
    n = batch * t_q
    n_past = 0 if k_past is None else k_past.shape[3] // KEY_BLOCK
    qb = min(t_q, SB_Q_BLOCK)
    n_kb = n_past + pl.cdiv(t_q, KEY_BLOCK)
    assert n_past % 2 == 0 and (t_q <= KEY_BLOCK or qb % (2 * KEY_BLOCK) == 0), (n_past, t_q, qb)
    npair = N_PAIRS if t_q <= KEY_BLOCK else 1
    steps = N_PAIRS // npair
    width = npair * LANES
    col = lambda section: pl.BlockSpec((t_q, width), lambda b, p: (b, section * steps + p))
    in_specs = [col(0), col(1), col(2)]
    args = [qkvg, qkvg, qkvg]
    if n_past:
        past = k_past.shape[3]
        cache = pl.BlockSpec((1, 2 * npair, D_HEAD, past), lambda b, p: (b, p, 0, 0))
        in_specs += [cache, cache]
        args += [k_past, v_past]
    in_specs.append(pl.BlockSpec((2 * KEY_BLOCK, 4 * KEY_BLOCK), lambda b, p: (0, 0)))
    args.append(_cumsum_matrix())
    kv_out = pl.BlockSpec((1, 2 * npair, D_HEAD, t_q), lambda b, p: (b, p, 0, 0))
    return pl.pallas_call(
        functools.partial(_sb_kernel, t_q=t_q, qb=qb, n_past=n_past, npair=npair),
        grid=(batch, steps),
        in_specs=in_specs,
        out_specs=[pl.BlockSpec((t_q, width), lambda b, p: (b, p)), kv_out, kv_out],
        out_shape=[jax.ShapeDtypeStruct((n, D_MODEL), F32),
                   jax.ShapeDtypeStruct((batch, N_HEADS, D_HEAD, t_q), F32),
                   jax.ShapeDtypeStruct((batch, N_HEADS, D_HEAD, t_q), F32)],
        scratch_shapes=[pltpu.VMEM((t_q, width), BF16),
                        pltpu.VMEM((npair, n_kb, LANES, 2 * KEY_BLOCK), BF16),
                        pltpu.VMEM((npair, n_kb, 2 * KEY_BLOCK, LANES), BF16),
                        pltpu.VMEM((npair, qb, 2 * KEY_BLOCK), F32),
                        pltpu.VMEM((qb, width), F32)],
        compiler_params=pltpu.CompilerParams(
            dimension_semantics=("parallel", "parallel"), vmem_limit_bytes=VMEM_LIMIT),
        name="sb",
    )(*args)


def _rwkv_kernel(pb_ref, sh_ref, s0_ref, mu_ref, w0_ref, w2_ref, a0_ref, a2_ref, kk_ref, ka_ref,
                 rk_ref, lg_ref, lb_ref, bd_ref, tri_ref,
                 og_ref, so_ref, s_scr, prev_scr, *, gb, rows):
    ch = RWKV_CHUNK
    c = pl.program_id(1)
    bd = bd_ref[...]
    tri = tri_ref[...]

    lane = lax.broadcasted_iota(jnp.int32, (ch, LANES), 1)
    row = lax.broadcasted_iota(jnp.int32, (ch, LANES), 0)
    first_head = lane < D_HEAD
    t_in = lane % D_HEAD
    strict = t_in < row
    incl = t_in <= row
    eye2 = jnp.where(t_in == row, 1.0, 0.0)
    same_head = (lax.broadcasted_iota(jnp.int32, (LANES, LANES), 0) // D_HEAD
                 == lax.broadcasted_iota(jnp.int32, (LANES, LANES), 1) // D_HEAD)

    def blockdiag(y):
        return jnp.concatenate([jnp.where(first_head, y, 0.0), jnp.where(first_head, 0.0, y)],
                               axis=0).astype(BF16)

    def pair_dot(x, y):
        return _dot(x.astype(BF16), blockdiag(y))

    def dot_nt(x, y):
        return lax.dot_general(x, y, (((1,), (1,)), ((), ())), preferred_element_type=F32)

    @pl.when(c == 0)
    def _():
        zero = jnp.zeros((D_HEAD, D_HEAD), F32)
        for b in range(gb):
            prev_scr[b] = sh_ref[b]
            for j in range(N_PAIRS):
                s_scr[b, j] = jnp.concatenate(
                    [jnp.concatenate([s0_ref[b, 2 * j], zero], axis=1),
                     jnp.concatenate([zero, s0_ref[b, 2 * j + 1]], axis=1)], axis=0)

    wide_row = lax.broadcasted_iota(jnp.int32, (ch, C_SHIFT), 0)
    live = lax.broadcasted_iota(jnp.int32, (ch, D_MODEL), 0) < rows
    tok = []
    for b in range(gb):
        p = pb_ref[b]
        if rows < ch:
            p = jnp.concatenate([p, jnp.zeros((ch - rows, C_SHIFT), F32)], axis=0)
        p_prev = jnp.where(wide_row == 0, prev_scr[b], pltpu.roll(p, 1, 0))
        prev_scr[b] = p[rows - 1:rows, :]
        pm = p + mu_ref[...] * (p_prev - p)
        r = pm[:, 0:D_MODEL]
        kb = pm[:, D_MODEL:2 * D_MODEL]
        vb = pm[:, 2 * D_MODEL:3 * D_MODEL]
        lora = pm[:, LORA_OFF:LORA_OFF + LANES]
        zb = pm[:, ZB_OFF:ZB_OFF + D_MODEL]
        w_raw = -_softplus(-(w0_ref[...] + _dot(jnp.tanh(lora).astype(BF16), w2_ref[...]))) - 0.5
        lw = -jnp.exp(w_raw)
        a = _sigmoid(a0_ref[...] + _dot(lora.astype(BF16), a2_ref[...]))
        kk = kb * kk_ref[...]
        kk = kk * lax.rsqrt(jnp.maximum(_head_sum_wide(kk * kk, bd), 1e-24))
        k2 = kb * (1.0 + (a - 1.0) * ka_ref[...])
        if rows < ch:
            r, kk, k2, vb, lw = (jnp.where(live, x, 0.0) for x in (r, kk, k2, vb, lw))
        l1 = lw.astype(BF16)
        rem = lw - l1.astype(F32)
        l2 = rem.astype(BF16)
        l3 = (rem - l2.astype(F32)).astype(BF16)
        tok.append(dict(r=r, kk=kk, k2=k2, vb=vb, lw=lw, be=kk * a, zb=zb,
                        bonus=_head_sum_wide(r * k2 * rk_ref[...], bd) * vb,
                        cs=_dot(tri, l1) + _dot(tri, l2) + _dot(tri, l3)))

    chains = [(b, j) for b in range(gb) for j in range(N_PAIRS)]
    pairs = range(len(chains))
    tile = lambda name, q: tok[chains[q][0]][name][:, chains[q][1] * LANES:(chains[q][1] + 1) * LANES]
    cs_end = [tile("cs", q)[ch - 1:ch, :] for q in pairs]
    g_inv = [jnp.exp(-tile("cs", q)) for q in pairs]
    x = [jnp.concatenate([tile("kk", q) * jnp.exp(tile("cs", q) - tile("lw", q)),
                          tile("r", q) * jnp.exp(tile("cs", q))], axis=0).astype(BF16) for q in pairs]
    gram = [dot_nt(x[q], jnp.concatenate([blockdiag(tile("be", q) * g_inv[q]),
                                          blockdiag(tile("k2", q) * g_inv[q])], axis=0)) for q in pairs]
    l_ab = [jnp.where(strict, gram[q][:ch, :LANES], 0.0) for q in pairs]
    m_ak = [jnp.where(strict, gram[q][:ch, LANES:], 0.0).astype(BF16) for q in pairs]
    m_r = [jnp.concatenate([jnp.where(incl, gram[q][ch:, :LANES], 0.0),
                            jnp.where(incl, gram[q][ch:, LANES:], 0.0)], axis=1).astype(BF16)
           for q in pairs]
    inv = [eye2 - l_ab[q] for q in pairs]
    pw = [pair_dot(l_ab[q], l_ab[q]) for q in pairs]
    n_prod = ch.bit_length() - 2
    for it in range(n_prod):
        if it < n_prod - 1:
            both = [pair_dot(jnp.concatenate([inv[q], pw[q]], axis=0), pw[q]) for q in pairs]
            inv = [inv[q] + both[q][:ch] for q in pairs]
            pw = [both[q][ch:] for q in pairs]
        else:
            inv = [inv[q] + pair_dot(inv[q], pw[q]) for q in pairs]
    s_bd = [s_scr[chains[q]] for q in pairs]
    w1 = [dot_nt(x[q], s_bd[q].astype(BF16)) for q in pairs]
    v_bd = [blockdiag(tile("vb", q)) for q in pairs]
    z = [w1[q][:ch] + _dot(m_ak[q], v_bd[q]) for q in pairs]
    u = [-pair_dot(inv[q], z[q]) for q in pairs]
    outs = [w1[q][ch:] + _dot(m_r[q], jnp.concatenate([blockdiag(u[q]), v_bd[q]], axis=0))
            for q in pairs]
    for q in pairs:
        g_end = jnp.exp(cs_end[q] - tile("cs", q))
        uv_t = jnp.concatenate([u[q], tile("vb", q)], axis=0).T.astype(BF16)
        bk = jnp.concatenate([tile("be", q) * g_end, tile("k2", q) * g_end], axis=0).astype(BF16)
        s_scr[chains[q]] = s_bd[q] * jnp.exp(cs_end[q]) + jnp.where(same_head, _dot(uv_t, bk), 0.0)

    for b in range(gb):
        o = jnp.concatenate(outs[b * N_PAIRS:(b + 1) * N_PAIRS], axis=1)
        mean = _head_sum_wide(o, bd) * (1.0 / D_HEAD)
        dlt = o - mean
        var = _head_sum_wide(dlt * dlt, bd) * (1.0 / D_HEAD)
        o = dlt * lax.rsqrt(var + LNX_EPS) * lg_ref[...] + lb_ref[...] + tok[b]["bonus"]
        zb = tok[b]["zb"]
        og_ref[b] = (o * (zb * _sigmoid(zb)))[:rows].astype(BF16)

    @pl.when(c == pl.num_programs(1) - 1)
    def _():
        for b in range(gb):
            for j in range(N_PAIRS):
                so_ref[b, 2 * j] = s_scr[b, j, :D_HEAD, :D_HEAD]
                so_ref[b, 2 * j + 1] = s_scr[b, j, D_HEAD:, D_HEAD:]


def _rwkv(pb, shift_prev, wkv_prev, prm, gb):
    batch, t, _ = pb.shape
    rows = min(t, RWKV_CHUNK)
    row = lambda v: v.reshape(1, -1)
    const = lambda shape: pl.BlockSpec(shape, lambda g, c: (0,) * len(shape))
    zeros = jnp.zeros((D_HEAD, D_MODEL), F32)
    w2p = jnp.concatenate([prm["w2"], zeros], axis=0).astype(BF16)
    a2p = jnp.concatenate([zeros, prm["a2"]], axis=0).astype(BF16)
    head = jnp.arange(LANES) // D_HEAD
    bd = (head[:, None] == head[None, :]).astype(BF16)
    tok = jnp.arange(RWKV_CHUNK)
    tri = (tok[:, None] >= tok[None, :]).astype(BF16)
    vec_spec = const((1, D_MODEL))
    return pl.pallas_call(
        functools.partial(_rwkv_kernel, gb=gb, rows=rows),
        grid=(batch // gb, t // rows),
        in_specs=[pl.BlockSpec((gb, rows, C_SHIFT), lambda g, c: (g, c, 0)),
                  pl.BlockSpec((gb, 1, C_SHIFT), lambda g, c: (g, 0, 0)),
                  pl.BlockSpec((gb, N_HEADS, D_HEAD, D_HEAD), lambda g, c: (g, 0, 0, 0)),
                  const((1, C_SHIFT)), vec_spec, const((LANES, D_MODEL)), vec_spec,
                  const((LANES, D_MODEL)), vec_spec, vec_spec, vec_spec, vec_spec, vec_spec,
                  const((LANES, LANES)), const((RWKV_CHUNK, RWKV_CHUNK))],
        out_specs=[pl.BlockSpec((gb, rows, D_MODEL), lambda g, c: (g, c, 0)),
                   pl.BlockSpec((gb, N_HEADS, D_HEAD, D_HEAD), lambda g, c: (g, 0, 0, 0))],
        out_shape=[jax.ShapeDtypeStruct((batch, t, D_MODEL), BF16),
                   jax.ShapeDtypeStruct((batch, N_HEADS, D_HEAD, D_HEAD), F32)],
        scratch_shapes=[pltpu.VMEM((gb, N_PAIRS, LANES, LANES), F32),
                        pltpu.VMEM((gb, 1, C_SHIFT), F32)],
        compiler_params=pltpu.CompilerParams(
            dimension_semantics=("parallel", "arbitrary"), vmem_limit_bytes=VMEM_LIMIT),
        name="rwkv",
    )(pb, shift_prev, wkv_prev, row(prm["mu_shift"]), row(prm["w0"]), w2p, row(prm["a0"]), a2p,
      row(prm["k_k"]), row(prm["k_a"]), row(prm["r_k"]), row(prm["lnx_g"]), row(prm["lnx_b"]), bd, tri)


def _out_kernel(x_ref, za_ref, ga_ref, gb_ref, oa_ref, ob_ref, woa_ref, wob_ref, wout_ref, fg_ref,
                y_ref, *, final_norm):
    za = za_ref[...]
    y_a = _dot((oa_ref[...] * (za * _sigmoid(za))).astype(BF16), woa_ref[...])
    y_b = _dot(ob_ref[...], wob_ref[...])
    merged = _sigmoid(ga_ref[...]) * y_a + _sigmoid(gb_ref[...]) * y_b
    y = x_ref[...] + _dot(merged.astype(BF16), wout_ref[...])
    if final_norm:
        ms = jnp.mean(y * y, axis=-1, keepdims=True)
        y = y * lax.rsqrt(ms + EPS) * fg_ref[...]
    y_ref[...] = y


def _out(x2, qkvg, o_a, og_b, w_o_a, w_o_b, w_out, final_g, tm, final_norm):
    n, d = x2.shape
    tok = lambda blk: pl.BlockSpec((tm, d), lambda i: (i, blk))
    wspec = pl.BlockSpec((d, d), lambda i: (0, 0))
    return pl.pallas_call(
        functools.partial(_out_kernel, final_norm=final_norm),
        grid=(n // tm,),
        in_specs=[tok(0), tok(3), tok(4), tok(5), tok(0), tok(0), wspec, wspec, wspec,
                  pl.BlockSpec((1, d), lambda i: (0, 0))],
        out_specs=tok(0),
        out_shape=jax.ShapeDtypeStruct((n, d), F32),
        compiler_params=pltpu.CompilerParams(
            dimension_semantics=("parallel",), vmem_limit_bytes=VMEM_LIMIT),
        name="out",
    )(x2, qkvg, qkvg, qkvg, o_a, og_b, w_o_a, w_o_b, w_out, final_g)


def _layer(x, shift_prev, k_past, v_past, wkv_prev, prm, final_g, final_norm):
    batch, t, d = x.shape
    n = batch * t
    x2 = x.reshape(n, d)
    g = prm["norm_g"].reshape(1, d)
    tm = min(n, PROJ_ROWS)
    qkvg = _proj(x2, g, prm["w_in_a"], tm, 2 * D_MODEL)
    pb = _proj(x2, g, prm["w_in_b"], tm, C_SHIFT // 3).reshape(batch, t, C_SHIFT)
    dim_major = lambda c: None if c is None else jnp.swapaxes(c, -1, -2)
    o_a, k_new, v_new = _sb(qkvg, dim_major(k_past), dim_major(v_past), batch, t)
    og_b, wkv_new = _rwkv(pb, shift_prev, wkv_prev, prm, RWKV_GROUP)
    y = _out(x2, qkvg, o_a, og_b.reshape(n, d), prm["w_o_a"], prm["w_o_b"], prm["w_out"],
             final_g.reshape(1, d), min(n, OUT_ROWS), final_norm)
    return (y.reshape(batch, t, d), jnp.swapaxes(k_new, -1, -2), jnp.swapaxes(v_new, -1, -2),
            pb[:, t - 1:, :], wkv_new)


def kernel(x_prompt, x_sample, cache_sb_k, cache_sb_v, state_shift, state_wkv, norm_g, w_in, mu_shift,
           w0, w2, a0, a2, k_k, k_a, r_k, lnx_g, lnx_b, w_o_a, w_o_b, w_out, final_norm_g):
    depth = w_in.shape[0]
    batch = x_prompt.shape[0]
    xp, xs = x_prompt, x_sample
    outs_p, outs_s = [], []
    for l in range(depth):
        prm = dict(norm_g=norm_g[l],
                   w_in_a=w_in[l][:, :C_QKVG].astype(BF16), w_in_b=w_in[l][:, C_QKVG:].astype(BF16),
                   mu_shift=mu_shift[l], w0=w0[l], w2=w2[l], a0=a0[l], a2=a2[l], k_k=k_k[l], k_a=k_a[l],
                   r_k=r_k[l].reshape(-1), lnx_g=lnx_g[l], lnx_b=lnx_b[l],
                   w_o_a=w_o_a[l].astype(BF16), w_o_b=w_o_b[l].astype(BF16), w_out=w_out[l].astype(BF16))
        last = l == depth - 1
        zero_shift = jnp.zeros((batch, 1, C_SHIFT), F32)
        zero_wkv = jnp.zeros((batch, N_HEADS, D_HEAD, D_HEAD), F32)
        xp, *rest_p = _layer(xp, zero_shift, None, None, zero_wkv, prm, final_norm_g, last)
        xs, *rest_s = _layer(xs, state_shift[l], cache_sb_k[l], cache_sb_v[l], state_wkv[l],
                             prm, final_norm_g, last)
        outs_p.append(rest_p)
        outs_s.append(rest_s)
    stack = lambda outs, i: jnp.stack([o[i] for o in outs])
    return (xp, xs,
            stack(outs_p, 0), stack(outs_p, 1), stack(outs_p, 2), stack(outs_p, 3),
            stack(outs_s, 0), stack(outs_s, 1), stack(outs_s, 2), stack(outs_s, 3))
```

```python
import functools

import jax
import jax.numpy as jnp
from jax import lax
from jax.experimental import pallas as pl
from jax.experimental.pallas import tpu as pltpu

F32 = jnp.float32
BF16 = jnp.bfloat16

D_MODEL = 1024
N_HEADS = 16
D_HEAD = 64
LANES = 128
SUBLANES = 8
N_PAIRS = N_HEADS // 2
C_QKVG = 6 * D_MODEL
C_SHIFT = 3 * D_MODEL + 2 * D_HEAD + D_MODEL
LORA_OFF = 3 * D_MODEL
ZB_OFF = LORA_OFF + LANES
EPS = 1e-6
LNX_EPS = 64e-5
PROJ_ROWS = 1024
OUT_ROWS = 512
KEY_BLOCK = 128
SB_Q_BLOCK = 256
EXP_ZERO = -110.0
RWKV_GROUP = 4
RWKV_CHUNK = 64
VMEM_LIMIT = 56 * 1024 * 1024


def _softplus(x):
    return jnp.maximum(x, 0.0) + jnp.log(1.0 + jnp.exp(-jnp.abs(x)))


def _sigmoid(x):
    return 1.0 / (1.0 + jnp.exp(-x))


def _dot(a, b):
    return jnp.dot(a, b, preferred_element_type=F32)


def _head_sum(x, bd):
    return _dot(x.astype(BF16), bd)


def _head_sum_wide(x, bd):
    return jnp.concatenate(
        [_head_sum(x[:, j * LANES:(j + 1) * LANES], bd) for j in range(x.shape[1] // LANES)], axis=1)


def _proj_kernel(x_ref, g_ref, w_ref, o_ref, h_scr):
    @pl.when(pl.program_id(1) == 0)
    def _():
        x = x_ref[...]
        ms = jnp.mean(x * x, axis=-1, keepdims=True)
        h_scr[...] = (x * lax.rsqrt(ms + EPS) * g_ref[...]).astype(BF16)

    o_ref[...] = _dot(h_scr[...], w_ref[...])


def _proj(x2, g, w, tm, tn):
    n, d = x2.shape
    n_out = w.shape[1]
    return pl.pallas_call(
        _proj_kernel,
        grid=(n // tm, n_out // tn),
        in_specs=[pl.BlockSpec((tm, d), lambda i, j: (i, 0)),
                  pl.BlockSpec((1, d), lambda i, j: (0, 0)),
                  pl.BlockSpec((d, tn), lambda i, j: (0, j))],
        out_specs=pl.BlockSpec((tm, tn), lambda i, j: (i, j)),
        out_shape=jax.ShapeDtypeStruct((n, n_out), F32),
        scratch_shapes=[pltpu.VMEM((tm, d), BF16)],
        compiler_params=pltpu.CompilerParams(
            dimension_semantics=("parallel", "arbitrary"), vmem_limit_bytes=VMEM_LIMIT),
        name="proj",
    )(x2, g, w)


def _sb_kernel(*refs, t_q, qb, n_past, npair, nq):
    if n_past:
        (q_ref, k_ref, v_ref, kc_ref, vc_ref, mc_ref,
         o_ref, ko_ref, vo_ref, q2_scr, k2_scr, v2_scr, later_scr, acc_scr) = refs
    else:
        (q_ref, k_ref, v_ref, mc_ref,
         o_ref, ko_ref, vo_ref, q2_scr, k2_scr, v2_scr, later_scr, acc_scr) = refs
    pairs = range(npair)
    tile = lambda g: slice(g * LANES, (g + 1) * LANES)

    lane = lax.broadcasted_iota(jnp.int32, (KEY_BLOCK, LANES), 1)
    sub = lax.broadcasted_iota(jnp.int32, (LANES, KEY_BLOCK), 0)
    first_head = lane < D_HEAD
    first_head_rows = sub < D_HEAD

    q2_scr[...] = (q_ref[...] * (D_HEAD ** -0.5)).astype(BF16)
    for g in pairs:
        kp = k_ref[:, tile(g)]
        vp = v_ref[:, tile(g)]
        if t_q < KEY_BLOCK:
            pad = jnp.zeros((KEY_BLOCK - t_q, LANES), F32)
            kp = jnp.concatenate([kp, pad], axis=0)
            vp = jnp.concatenate([vp, pad], axis=0)

        kpt = kp.T
        vpt = vp.T
        for h in range(2):
            ko_ref[0, 2 * g + h] = kpt[h * D_HEAD:(h + 1) * D_HEAD, :t_q]
            vo_ref[0, 2 * g + h] = vpt[h * D_HEAD:(h + 1) * D_HEAD, :t_q]

        def put_block(j, ktj, vj):
            k2_scr[g, j, :, :KEY_BLOCK] = jnp.where(first_head_rows, ktj, 0.0).astype(BF16)
            k2_scr[g, j, :, KEY_BLOCK:] = jnp.where(first_head_rows, 0.0, ktj).astype(BF16)
            v2_scr[g, j, :KEY_BLOCK] = jnp.where(first_head, vj, 0.0).astype(BF16)
            v2_scr[g, j, KEY_BLOCK:] = jnp.where(first_head, 0.0, vj).astype(BF16)

        for j in range(n_past):
            cols = slice(j * KEY_BLOCK, (j + 1) * KEY_BLOCK)
            put_block(j,
                      jnp.concatenate([kc_ref[0, 2 * g, :, cols], kc_ref[0, 2 * g + 1, :, cols]], axis=0),
                      jnp.concatenate([vc_ref[0, 2 * g, :, cols], vc_ref[0, 2 * g + 1, :, cols]], axis=0).T)
        for j in range(kp.shape[0] // KEY_BLOCK):
            cols = slice(j * KEY_BLOCK, (j + 1) * KEY_BLOCK)
            put_block(n_past + j, kpt[:, cols], vp[cols])

    row = lax.broadcasted_iota(jnp.int32, (qb, 2 * KEY_BLOCK), 0)
    col = lax.broadcasted_iota(jnp.int32, (qb, 2 * KEY_BLOCK), 1) % KEY_BLOCK
    earlier = col < row
    mc = mc_ref[...]

    def suffix_sums(lkb):
        return _dot(lkb, mc)
    n_sub = max(qb // KEY_BLOCK, 1)

    lowest = jnp.float32(-3.0e38)
    chains = [(e, g) for e in range(nq) for g in pairs]

    def q_group(u, carry):
        q0 = [pl.multiple_of((u * nq + e) * qb, qb) for e in range(nq)]
        first = [n_past + (u * nq + e) * n_sub for e in range(nq)]

        def attend(r0):
            m = qb - r0
            d = r0 // KEY_BLOCK
            z = [_dot(q2_scr[pl.ds(q0[e] + r0, m), tile(g)], k2_scr[g, first[e] + d]) for e, g in chains]
            sp = [_softplus(x) for x in z]
            lkb = [jnp.where(earlier[:m], -s, 0.0).astype(BF16) for s in sp]
            sums = [suffix_sums(x) for x in lkb]
            for c, (e, g) in enumerate(chains):
                later = later_scr[e, g, r0:, :]
                w = jnp.exp(z[c] - sp[c] + sums[c][:, :2 * KEY_BLOCK] + later)
                w = jnp.where(earlier[:m], w, 0.0).astype(BF16)
                later_scr[e, g, r0:, :] = later + sums[c][:, 2 * KEY_BLOCK:]
                acc_scr[e, r0:, tile(g)] = acc_scr[e, r0:, tile(g)] + _dot(w, v2_scr[g, first[e] + d])

        def attend_two():
            kb = KEY_BLOCK
            zero = jnp.zeros((kb, 2 * kb), F32)
            z_new = [_dot(q2_scr[pl.ds(q0[e] + kb, kb), tile(g)], k2_scr[g, first[e] + 1]) for e, g in chains]
            z_old = [_dot(q2_scr[pl.ds(q0[e], qb), tile(g)], k2_scr[g, first[e]]) for e, g in chains]
            sp_new = [_softplus(x) for x in z_new]
            sp_old = [_softplus(x) for x in z_old]
            lkb = [jnp.concatenate([jnp.where(earlier[:kb], -a, 0.0).astype(BF16),
                                    jnp.where(earlier, -b, 0.0).astype(BF16)], axis=0)
                   for a, b in zip(sp_new, sp_old)]
            sums = [suffix_sums(x) for x in lkb]
            for c, (e, g) in enumerate(chains):
                w_new = jnp.exp(z_new[c] - sp_new[c] + sums[c][:kb, :2 * kb])
                w_new = jnp.where(earlier[:kb], w_new, 0.0).astype(BF16)
                later = jnp.concatenate([zero, sums[c][:kb, 2 * kb:]], axis=0)
                w_old = jnp.exp(z_old[c] - sp_old[c] + sums[c][kb:, :2 * kb] + later)
                w_old = jnp.where(earlier, w_old, 0.0).astype(BF16)
                later_scr[e, g] = later + sums[c][kb:, 2 * kb:]
                acc_new = _dot(w_new, v2_scr[g, first[e] + 1])
                acc_scr[e, :, tile(g)] = _dot(w_old, v2_scr[g, first[e]]) + jnp.concatenate(
                    [jnp.zeros((kb, LANES), F32), acc_new], axis=0)

        if n_sub == 2:
            attend_two()
        else:
            later_scr[...] = jnp.zeros_like(later_scr)
            acc_scr[...] = jnp.zeros_like(acc_scr)
            for d in reversed(range(n_sub)):
                attend(d * KEY_BLOCK)

        def pending(e, s):
            return first[e] - 1 - 2 * s >= 0

        def reach(e, s):
            top = functools.reduce(jnp.maximum, [jnp.max(later_scr[e, g]) for g in pairs])
            return jnp.where(pending(e, s), top, lowest)

        def more(c):
            return c[1] > EXP_ZERO

        def older(c):
            s = c[0]
            j = [jnp.maximum(first[e] - 1 - 2 * s, 1) for e in range(nq)]
            gone = [jnp.where(pending(e, s), 0.0, lowest) for e in range(nq)]
            z = [_dot(q2_scr[pl.ds(q0[e], qb), tile(g)], k2_scr[g, j[e] - d]) for e, g in chains for d in range(2)]
            sp = [_softplus(x) for x in z]
            lkb = [(-x).astype(BF16) for x in sp]
            sums = [suffix_sums(jnp.concatenate(lkb[2 * c:2 * c + 2], axis=0)) for c in range(len(chains))]
            for c, (e, g) in enumerate(chains):
                later = [later_scr[e, g] + gone[e]]
                w = []
                for d in range(2):
                    rows = slice(d * qb, (d + 1) * qb)
                    w.append(jnp.exp(z[2 * c + d] - sp[2 * c + d] + sums[c][rows, :2 * KEY_BLOCK]
                                     + later[d]).astype(BF16))
                    later.append(later[d] + sums[c][rows, 2 * KEY_BLOCK:])
                later_scr[e, g] = later[2]
                v_two = v2_scr[g, pl.ds(j[e] - 1, 2)].reshape(4 * KEY_BLOCK, LANES)
                acc_scr[e, :, tile(g)] = acc_scr[e, :, tile(g)] + _dot(jnp.concatenate([w[1], w[0]], axis=1), v_two)
            return s + 1, functools.reduce(jnp.maximum, [reach(e, s + 1) for e in range(nq)])

        lax.while_loop(more, older, older((0, None)))
        for e in range(nq):
            o_ref[pl.ds(q0[e], qb), :] = acc_scr[e]
        return carry

    lax.fori_loop(0, t_q // (qb * nq), q_group, 0)


def _cumsum_matrix():
    s = jnp.arange(KEY_BLOCK)
    after = (s[:, None] > s[None, :]).astype(F32)
    ones = jnp.ones((KEY_BLOCK, KEY_BLOCK), F32)
    z = jnp.zeros((KEY_BLOCK, KEY_BLOCK), F32)
    bd = lambda m: jnp.concatenate([jnp.concatenate([m, z], 1), jnp.concatenate([z, m], 1)], 0)
    return jnp.concatenate([bd(after), bd(ones)], axis=1).astype(BF16)


def _sb(qkvg, k_past, v_past, batch, t_q):
    n = batch * t_q
    n_past = 0 if k_past is None else k_past.shape[3] // KEY_BLOCK
    qb = min(t_q, SB_Q_BLOCK)
    n_kb = n_past + pl.cdiv(t_q, KEY_BLOCK)
    assert n_past % 2 == 0 and (t_q <= KEY_BLOCK or qb % (2 * KEY_BLOCK) == 0), (n_past, t_q, qb)
    npair = N_PAIRS if t_q <= KEY_BLOCK else 1
    nq = 2 if (t_q // qb) % 2 == 0 else 1
    steps = N_PAIRS // npair
    width = npair * LANES
    col = lambda section: pl.BlockSpec((t_q, width), lambda b, p: (b, section * steps + p))
    in_specs = [col(0), col(1), col(2)]
    args = [qkvg, qkvg, qkvg]
    if n_past:
        past = k_past.shape[3]
        cache = pl.BlockSpec((1, 2 * npair, D_HEAD, past), lambda b, p: (b, p, 0, 0))
        in_specs += [cache, cache]
        args += [k_past, v_past]
    in_specs.append(pl.BlockSpec((2 * KEY_BLOCK, 4 * KEY_BLOCK), lambda b, p: (0, 0)))
    args.append(_cumsum_matrix())
    kv_out = pl.BlockSpec((1, 2 * npair, D_HEAD, t_q), lambda b, p: (b, p, 0, 0))
    return pl.pallas_call(
        functools.partial(_sb_kernel, t_q=t_q, qb=qb, n_past=n_past, npair=npair, nq=nq),
        grid=(batch, steps),
        in_specs=in_specs,
        out_specs=[pl.BlockSpec((t_q, width), lambda b, p: (b, p)), kv_out, kv_out],
        out_shape=[jax.ShapeDtypeStruct((n, D_MODEL), F32),
                   jax.ShapeDtypeStruct((batch, N_HEADS, D_HEAD, t_q), F32),
                   jax.ShapeDtypeStruct((batch, N_HEADS, D_HEAD, t_q), F32)],
        scratch_shapes=[pltpu.VMEM((t_q, width), BF16),
                        pltpu.VMEM((npair, n_kb, LANES, 2 * KEY_BLOCK), BF16),
                        pltpu.VMEM((npair, n_kb, 2 * KEY_BLOCK, LANES), BF16),
                        pltpu.VMEM((nq, npair, qb, 2 * KEY_BLOCK), F32),
                        pltpu.VMEM((nq, qb, width), F32)],
        compiler_params=pltpu.CompilerParams(
            dimension_semantics=("parallel", "parallel"), vmem_limit_bytes=VMEM_LIMIT),
        name="sb",
    )(*args)


def _rwkv_kernel(pb_ref, sh_ref, s0_ref, mu_ref, w0_ref, w2_ref, a0_ref, a2_ref, kk_ref, ka_ref,
                 rk_ref, lg_ref, lb_ref, bd_ref, tri_ref,
                 og_ref, so_ref, s_scr, prev_scr, *, gb, rows):
    ch = RWKV_CHUNK
    c = pl.program_id(1)
    bd = bd_ref[...]
    tri = tri_ref[...]

    lane = lax.broadcasted_iota(jnp.int32, (ch, LANES), 1)
    row = lax.broadcasted_iota(jnp.int32, (ch, LANES), 0)
    first_head = lane < D_HEAD
    t_in = lane % D_HEAD
    strict = t_in < row
    incl = t_in <= row
    eye2 = jnp.where(t_in == row, 1.0, 0.0)
    same_head = (lax.broadcasted_iota(jnp.int32, (LANES, LANES), 0) // D_HEAD
                 == lax.broadcasted_iota(jnp.int32, (LANES, LANES), 1) // D_HEAD)

    def blockdiag(y):
        return jnp.concatenate([jnp.where(first_head, y, 0.0), jnp.where(first_head, 0.0, y)],
                               axis=0).astype(BF16)

    def pair_dot(x, y):
        return _dot(x.astype(BF16), blockdiag(y))

    def dot_nt(x, y):
        return lax.dot_general(x, y, (((1,), (1,)), ((), ())), preferred_element_type=F32)

    @pl.when(c == 0)
    def _():
        zero = jnp.zeros((D_HEAD, D_HEAD), F32)
        for b in range(gb):
            prev_scr[b] = sh_ref[b]
            for j in range(N_PAIRS):
                s_scr[b, j] = jnp.concatenate(
                    [jnp.concatenate([s0_ref[b, 2 * j], zero], axis=1),
                     jnp.concatenate([zero, s0_ref[b, 2 * j + 1]], axis=1)], axis=0)

    wide_row = lax.broadcasted_iota(jnp.int32, (ch, C_SHIFT), 0)
    live = lax.broadcasted_iota(jnp.int32, (ch, D_MODEL), 0) < rows
    tok = []
    for b in range(gb):
        p = pb_ref[b]
        if rows < ch:
            p = jnp.concatenate([p, jnp.zeros((ch - rows, C_SHIFT), F32)], axis=0)
        p_prev = jnp.where(wide_row == 0, prev_scr[b], pltpu.roll(p, 1, 0))
        prev_scr[b] = p[rows - 1:rows, :]
        pm = p + mu_ref[...] * (p_prev - p)
        r = pm[:, 0:D_MODEL]
        kb = pm[:, D_MODEL:2 * D_MODEL]
        vb = pm[:, 2 * D_MODEL:3 * D_MODEL]
        lora = pm[:, LORA_OFF:LORA_OFF + LANES]
        zb = pm[:, ZB_OFF:ZB_OFF + D_MODEL]
        w_raw = -_softplus(-(w0_ref[...] + _dot(jnp.tanh(lora).astype(BF16), w2_ref[...]))) - 0.5
        lw = -jnp.exp(w_raw)
        a = _sigmoid(a0_ref[...] + _dot(lora.astype(BF16), a2_ref[...]))
        kk = kb * kk_ref[...]
        kk = kk * lax.rsqrt(jnp.maximum(_head_sum_wide(kk * kk, bd), 1e-24))
        k2 = kb * (1.0 + (a - 1.0) * ka_ref[...])
        if rows < ch:
            r, kk, k2, vb, lw = (jnp.where(live, x, 0.0) for x in (r, kk, k2, vb, lw))
        l1 = lw.astype(BF16)
        rem = lw - l1.astype(F32)
        l2 = rem.astype(BF16)
        l3 = (rem - l2.astype(F32)).astype(BF16)
        tok.append(dict(r=r, kk=kk, k2=k2, vb=vb, lw=lw, be=kk * a, zb=zb,
                        bonus=_head_sum_wide(r * k2 * rk_ref[...], bd) * vb,
                        cs=_dot(tri, l1) + _dot(tri, l2) + _dot(tri, l3)))

    chains = [(b, j) for b in range(gb) for j in range(N_PAIRS)]
    pairs = range(len(chains))
    tile = lambda name, q: tok[chains[q][0]][name][:, chains[q][1] * LANES:(chains[q][1] + 1) * LANES]
    cs_end = [tile("cs", q)[ch - 1:ch, :] for q in pairs]
    g_inv = [jnp.exp(-tile("cs", q)) for q in pairs]
    x = [jnp.concatenate([tile("kk", q) * jnp.exp(tile("cs", q) - tile("lw", q)),
                          tile("r", q) * jnp.exp(tile("cs", q))], axis=0).astype(BF16) for q in pairs]
    gram = [dot_nt(x[q], jnp.concatenate([blockdiag(tile("be", q) * g_inv[q]),
                                          blockdiag(tile("k2", q) * g_inv[q])], axis=0)) for q in pairs]
    l_ab = [jnp.where(strict, gram[q][:ch, :LANES], 0.0) for q in pairs]
    m_ak = [jnp.where(strict, gram[q][:ch, LANES:], 0.0).astype(BF16) for q in pairs]
    m_r = [jnp.concatenate([jnp.where(incl, gram[q][ch:, :LANES], 0.0),
                            jnp.where(incl, gram[q][ch:, LANES:], 0.0)], axis=1).astype(BF16)
           for q in pairs]
    inv = [eye2 - l_ab[q] for q in pairs]
    pw = [pair_dot(l_ab[q], l_ab[q]) for q in pairs]
    n_prod = ch.bit_length() - 2
    for it in range(n_prod):
        if it < n_prod - 1:
            both = [pair_dot(jnp.concatenate([inv[q], pw[q]], axis=0), pw[q]) for q in pairs]
            inv = [inv[q] + both[q][:ch] for q in pairs]
            pw = [both[q][ch:] for q in pairs]
        else:
            inv = [inv[q] + pair_dot(inv[q], pw[q]) for q in pairs]
    s_bd = [s_scr[chains[q]] for q in pairs]
    w1 = [dot_nt(x[q], s_bd[q].astype(BF16)) for q in pairs]
    v_bd = [blockdiag(tile("vb", q)) for q in pairs]
    z = [w1[q][:ch] + _dot(m_ak[q], v_bd[q]) for q in pairs]
    u = [-pair_dot(inv[q], z[q]) for q in pairs]
    outs = [w1[q][ch:] + _dot(m_r[q], jnp.concatenate([blockdiag(u[q]), v_bd[q]], axis=0))
            for q in pairs]
    for q in pairs:
        g_end = jnp.exp(cs_end[q] - tile("cs", q))
        uv_t = jnp.concatenate([u[q], tile("vb", q)], axis=0).T.astype(BF16)
        bk = jnp.concatenate([tile("be", q) * g_end, tile("k2", q) * g_end], axis=0).astype(BF16)
        s_scr[chains[q]] = s_bd[q] * jnp.exp(cs_end[q]) + jnp.where(same_head, _dot(uv_t, bk), 0.0)

    for b in range(gb):
        o = jnp.concatenate(outs[b * N_PAIRS:(b + 1) * N_PAIRS], axis=1)
        mean = _head_sum_wide(o, bd) * (1.0 / D_HEAD)
        dlt = o - mean
        var = _head_sum_wide(dlt * dlt, bd) * (1.0 / D_HEAD)
        o = dlt * lax.rsqrt(var + LNX_EPS) * lg_ref[...] + lb_ref[...] + tok[b]["bonus"]
        zb = tok[b]["zb"]
        og_ref[b] = (o * (zb * _sigmoid(zb)))[:rows].astype(BF16)

    @pl.when(c == pl.num_programs(1) - 1)
    def _():
        for b in range(gb):
            for j in range(N_PAIRS):
                so_ref[b, 2 * j] = s_scr[b, j, :D_HEAD, :D_HEAD]
                so_ref[b, 2 * j + 1] = s_scr[b, j, D_HEAD:, D_HEAD:]


def _rwkv(pb, shift_prev, wkv_prev, prm, gb):
    batch, t, _ = pb.shape
    rows = min(t, RWKV_CHUNK)
    row = lambda v: v.reshape(1, -1)
    const = lambda shape: pl.BlockSpec(shape, lambda g, c: (0,) * len(shape))
    zeros = jnp.zeros((D_HEAD, D_MODEL), F32)
    w2p = jnp.concatenate([prm["w2"], zeros], axis=0).astype(BF16)
    a2p = jnp.concatenate([zeros, prm["a2"]], axis=0).astype(BF16)
    head = jnp.arange(LANES) // D_HEAD
    bd = (head[:, None] == head[None, :]).astype(BF16)
    tok = jnp.arange(RWKV_CHUNK)
    tri = (tok[:, None] >= tok[None, :]).astype(BF16)
    vec_spec = const((1, D_MODEL))
    return pl.pallas_call(
        functools.partial(_rwkv_kernel, gb=gb, rows=rows),
        grid=(batch // gb, t // rows),
        in_specs=[pl.BlockSpec((gb, rows, C_SHIFT), lambda g, c: (g, c, 0)),
                  pl.BlockSpec((gb, 1, C_SHIFT), lambda g, c: (g, 0, 0)),
                  pl.BlockSpec((gb, N_HEADS, D_HEAD, D_HEAD), lambda g, c: (g, 0, 0, 0)),
                  const((1, C_SHIFT)), vec_spec, const((LANES, D_MODEL)), vec_spec,
                  const((LANES, D_MODEL)), vec_spec, vec_spec, vec_spec, vec_spec, vec_spec,
                  const((LANES, LANES)), const((RWKV_CHUNK, RWKV_CHUNK))],
        out_specs=[pl.BlockSpec((gb, rows, D_MODEL), lambda g, c: (g, c, 0)),
                   pl.BlockSpec((gb, N_HEADS, D_HEAD, D_HEAD), lambda g, c: (g, 0, 0, 0))],
        out_shape=[jax.ShapeDtypeStruct((batch, t, D_MODEL), BF16),
                   jax.ShapeDtypeStruct((batch, N_HEADS, D_HEAD, D_HEAD), F32)],
        scratch_shapes=[pltpu.VMEM((gb, N_PAIRS, LANES, LANES), F32),
                        pltpu.VMEM((gb, 1, C_SHIFT), F32)],
        compiler_params=pltpu.CompilerParams(
            dimension_semantics=("parallel", "arbitrary"), vmem_limit_bytes=VMEM_LIMIT),
        name="rwkv",
    )(pb, shift_prev, wkv_prev, row(prm["mu_shift"]), row(prm["w0"]), w2p, row(prm["a0"]), a2p,
      row(prm["k_k"]), row(prm["k_a"]), row(prm["r_k"]), row(prm["lnx_g"]), row(prm["lnx_b"]), bd, tri)


def _out_kernel(x_ref, za_ref, ga_ref, gb_ref, oa_ref, ob_ref, woa_ref, wob_ref, wout_ref, fg_ref,
                y_ref, *, final_norm):
    za = za_ref[...]
    y_a = _dot((oa_ref[...] * (za * _sigmoid(za))).astype(BF16), woa_ref[...])
    y_b = _dot(ob_ref[...], wob_ref[...])
    merged = _sigmoid(ga_ref[...]) * y_a + _sigmoid(gb_ref[...]) * y_b
    y = x_ref[...] + _dot(merged.astype(BF16), wout_ref[...])
    if final_norm:
        ms = jnp.mean(y * y, axis=-1, keepdims=True)
        y = y * lax.rsqrt(ms + EPS) * fg_ref[...]
    y_ref[...] = y


def _out(x2, qkvg, o_a, og_b, w_o_a, w_o_b, w_out, final_g, tm, final_norm):
    n, d = x2.shape
    tok = lambda blk: pl.BlockSpec((tm, d), lambda i: (i, blk))
    wspec = pl.BlockSpec((d, d), lambda i: (0, 0))
    return pl.pallas_call(
        functools.partial(_out_kernel, final_norm=final_norm),
        grid=(n // tm,),
        in_specs=[tok(0), tok(3), tok(4), tok(5), tok(0), tok(0), wspec, wspec, wspec,
                  pl.BlockSpec((1, d), lambda i: (0, 0))],
        out_specs=tok(0),
        out_shape=jax.ShapeDtypeStruct((n, d), F32),
        compiler_params=pltpu.CompilerParams(
            dimension_semantics=("parallel",), vmem_limit_bytes=VMEM_LIMIT),
        name="out",
    )(x2, qkvg, qkvg, qkvg, o_a, og_b, w_o_a, w_o_b, w_out, final_g)


def _layer(x, shift_prev, k_past, v_past, wkv_prev, prm, final_g, final_norm):
    batch, t, d = x.shape
    n = batch * t
    x2 = x.reshape(n, d)
    g = prm["norm_g"].reshape(1, d)
    tm = min(n, PROJ_ROWS)
    qkvg = _proj(x2, g, prm["w_in_a"], tm, 2 * D_MODEL)
    pb = _proj(x2, g, prm["w_in_b"], tm, C_SHIFT // 3).reshape(batch, t, C_SHIFT)
    dim_major = lambda c: None if c is None else jnp.swapaxes(c, -1, -2)
    o_a, k_new, v_new = _sb(qkvg, dim_major(k_past), dim_major(v_past), batch, t)
    og_b, wkv_new = _rwkv(pb, shift_prev, wkv_prev, prm, RWKV_GROUP)
    y = _out(x2, qkvg, o_a, og_b.reshape(n, d), prm["w_o_a"], prm["w_o_b"], prm["w_out"],
             final_g.reshape(1, d), min(n, OUT_ROWS), final_norm)
    return (y.reshape(batch, t, d), jnp.swapaxes(k_new, -1, -2), jnp.swapaxes(v_new, -1, -2),
            pb[:, t - 1:, :], wkv_new)


def kernel(x_prompt, x_sample, cache_sb_k, cache_sb_v, state_shift, state_wkv, norm_g, w_in, mu_shift,
           w0, w2, a0, a2, k_k, k_a, r_k, lnx_g, lnx_b, w_o_a, w_o_b, w_out, final_norm_g):
    depth = w_in.shape[0]
    batch = x_prompt.shape[0]
    xp, xs = x_prompt, x_sample
    outs_p, outs_s = [], []
    for l in range(depth):
        prm = dict(norm_g=norm_g[l],
                   w_in_a=w_in[l][:, :C_QKVG].astype(BF16), w_in_b=w_in[l][:, C_QKVG:].astype(BF16),
                   mu_shift=mu_shift[l], w0=w0[l], w2=w2[l], a0=a0[l], a2=a2[l], k_k=k_k[l], k_a=k_a[l],
                   r_k=r_k[l].reshape(-1), lnx_g=lnx_g[l], lnx_b=lnx_b[l],
                   w_o_a=w_o_a[l].astype(BF16), w_o_b=w_o_b[l].astype(BF16), w_out=w_out[l].astype(BF16))
        last = l == depth - 1
        zero_shift = jnp.zeros((batch, 1, C_SHIFT), F32)
        zero_wkv = jnp.zeros((batch, N_HEADS, D_HEAD, D_HEAD), F32)
        xp, *rest_p = _layer(xp, zero_shift, None, None, zero_wkv, prm, final_norm_g, last)
        xs, *rest_s = _layer(xs, state_shift[l], cache_sb_k[l], cache_sb_v[l], state_wkv[l],
                             prm, final_norm_g, last)
        outs_p.append(rest_p)
        outs_s.append(rest_s)
    stack = lambda outs, i: jnp.stack([o[i] for o in outs])
    return (xp, xs,
            stack(outs_p, 0), stack(outs_p, 1), stack(outs_p, 2), stack(outs_p, 3),
            stack(outs_s, 0), stack(outs_s, 1), stack(outs_s, 2), stack(outs_s, 3))
```

```python
import functools

import jax
import jax.numpy as jnp
from jax import lax
from jax.experimental import pallas as pl
from jax.experimental.pallas import tpu as pltpu

F32 = jnp.float32
BF16 = jnp.bfloat16

D_MODEL = 1024
N_HEADS = 16
D_HEAD = 64
LANES = 128
SUBLANES = 8
N_PAIRS = N_HEADS // 2
C_QKVG = 6 * D_MODEL
C_SHIFT = 3 * D_MODEL + 2 * D_HEAD + D_MODEL
LORA_OFF = 3 * D_MODEL
ZB_OFF = LORA_OFF + LANES
EPS = 1e-6
LNX_EPS = 64e-5
DECAY_SCALE = 0.6065306597126334
PROJ_ROWS = 1024
OUT_ROWS = 512
KEY_BLOCK = 128
SB_Q_BLOCK = 256
EXP_ZERO = -110.0
RWKV_GROUP = 4
RWKV_CHUNK = 64
VMEM_LIMIT = 56 * 1024 * 1024


def _log_sigmoid(x):
    return jnp.minimum(x, 0.0) - jnp.log(1.0 + jnp.exp(-jnp.abs(x)))


def _sigmoid(x):
    return 1.0 / (1.0 + jnp.exp(-x))


def _dot(a, b):
    return jnp.dot(a, b, preferred_element_type=F32)


def _head_sum(x, bd):
    return _dot(x.astype(BF16), bd)


def _head_sum_wide(x, bd):
    return jnp.concatenate(
        [_head_sum(x[:, j * LANES:(j + 1) * LANES], bd) for j in range(x.shape[1] // LANES)], axis=1)


def _proj_kernel(x_ref, g_ref, w_ref, o_ref, h_scr):
    @pl.when(pl.program_id(1) == 0)
    def _():
        x = x_ref[...]
        ms = jnp.mean(x * x, axis=-1, keepdims=True)
        h_scr[...] = (x * lax.rsqrt(ms + EPS) * g_ref[...]).astype(BF16)

    o_ref[...] = _dot(h_scr[...], w_ref[...])


def _proj(x2, g, w, tm, tn):
    n, d = x2.shape
    n_out = w.shape[1]
    return pl.pallas_call(
        _proj_kernel,
        grid=(n // tm, n_out // tn),
        in_specs=[pl.BlockSpec((tm, d), lambda i, j: (i, 0)),
                  pl.BlockSpec((1, d), lambda i, j: (0, 0)),
                  pl.BlockSpec((d, tn), lambda i, j: (0, j))],
        out_specs=pl.BlockSpec((tm, tn), lambda i, j: (i, j)),
        out_shape=jax.ShapeDtypeStruct((n, n_out), F32),
        scratch_shapes=[pltpu.VMEM((tm, d), BF16)],
        compiler_params=pltpu.CompilerParams(
            dimension_semantics=("parallel", "arbitrary"), vmem_limit_bytes=VMEM_LIMIT),
        name="proj",
    )(x2, g, w)


def _sb_kernel(*refs, t_q, qb, n_past, npair, nq):
    if n_past:
        (q_ref, k_ref, v_ref, kc_ref, vc_ref, mc_ref,
         o_ref, ko_ref, vo_ref, q2_scr, k2_scr, v2_scr, later_scr, acc_scr) = refs
    else:
        (q_ref, k_ref, v_ref, mc_ref,
         o_ref, ko_ref, vo_ref, q2_scr, k2_scr, v2_scr, later_scr, acc_scr) = refs
    pairs = range(npair)
    tile = lambda g: slice(g * LANES, (g + 1) * LANES)

    lane = lax.broadcasted_iota(jnp.int32, (KEY_BLOCK, LANES), 1)
    sub = lax.broadcasted_iota(jnp.int32, (LANES, KEY_BLOCK), 0)
    first_head = lane < D_HEAD
    first_head_rows = sub < D_HEAD

    q2_scr[...] = (q_ref[...] * (D_HEAD ** -0.5)).astype(BF16)
    for g in pairs:
        kp = k_ref[:, tile(g)]
        vp = v_ref[:, tile(g)]
        if t_q < KEY_BLOCK:
            pad = jnp.zeros((KEY_BLOCK - t_q, LANES), F32)
            kp = jnp.concatenate([kp, pad], axis=0)
            vp = jnp.concatenate([vp, pad], axis=0)

        kpt = kp.T
        vpt = vp.T
        for h in range(2):
            ko_ref[0, 2 * g + h] = kpt[h * D_HEAD:(h + 1) * D_HEAD, :t_q]
            vo_ref[0, 2 * g + h] = vpt[h * D_HEAD:(h + 1) * D_HEAD, :t_q]

        def put_block(j, ktj, vj):
            k2_scr[g, j, :, :KEY_BLOCK] = jnp.where(first_head_rows, ktj, 0.0).astype(BF16)
            k2_scr[g, j, :, KEY_BLOCK:] = jnp.where(first_head_rows, 0.0, ktj).astype(BF16)
            v2_scr[g, j, :KEY_BLOCK] = jnp.where(first_head, vj, 0.0).astype(BF16)
            v2_scr[g, j, KEY_BLOCK:] = jnp.where(first_head, 0.0, vj).astype(BF16)

        for j in range(n_past):
            cols = slice(j * KEY_BLOCK, (j + 1) * KEY_BLOCK)
            put_block(j,
                      jnp.concatenate([kc_ref[0, 2 * g, :, cols], kc_ref[0, 2 * g + 1, :, cols]], axis=0),
                      jnp.concatenate([vc_ref[0, 2 * g, :, cols], vc_ref[0, 2 * g + 1, :, cols]], axis=0).T)
        for j in range(kp.shape[0] // KEY_BLOCK):
            cols = slice(j * KEY_BLOCK, (j + 1) * KEY_BLOCK)
            put_block(n_past + j, kpt[:, cols], vp[cols])

    row = lax.broadcasted_iota(jnp.int32, (qb, 2 * KEY_BLOCK), 0)
    col = lax.broadcasted_iota(jnp.int32, (qb, 2 * KEY_BLOCK), 1) % KEY_BLOCK
    earlier = col < row
    mc = mc_ref[...]

    def suffix_sums(lkb):
        return _dot(lkb, mc)
    n_sub = max(qb // KEY_BLOCK, 1)

    lowest = jnp.float32(-3.0e38)
    chains = [(e, g) for e in range(nq) for g in pairs]

    def q_group(u, carry):
        q0 = [pl.multiple_of((u * nq + e) * qb, qb) for e in range(nq)]
        first = [n_past + (u * nq + e) * n_sub for e in range(nq)]

        def attend(r0):
            m = qb - r0
            d = r0 // KEY_BLOCK
            z = [_dot(q2_scr[pl.ds(q0[e] + r0, m), tile(g)], k2_scr[g, first[e] + d]) for e, g in chains]
            ls = [_log_sigmoid(x) for x in z]
            lkb = [jnp.where(earlier[:m], a - x, 0.0).astype(BF16) for a, x in zip(ls, z)]
            sums = [suffix_sums(x) for x in lkb]
            for c, (e, g) in enumerate(chains):
                later = later_scr[e, g, r0:, :]
                w = jnp.exp(ls[c] + sums[c][:, :2 * KEY_BLOCK] + later)
                w = jnp.where(earlier[:m], w, 0.0).astype(BF16)
                later_scr[e, g, r0:, :] = later + sums[c][:, 2 * KEY_BLOCK:]
                acc_scr[e, r0:, tile(g)] = acc_scr[e, r0:, tile(g)] + _dot(w, v2_scr[g, first[e] + d])

        def attend_two():
            kb = KEY_BLOCK
            zero = jnp.zeros((kb, 2 * kb), F32)
            z_new = [_dot(q2_scr[pl.ds(q0[e] + kb, kb), tile(g)], k2_scr[g, first[e] + 1]) for e, g in chains]
            z_old = [_dot(q2_scr[pl.ds(q0[e], qb), tile(g)], k2_scr[g, first[e]]) for e, g in chains]
            ls_new = [_log_sigmoid(x) for x in z_new]
            ls_old = [_log_sigmoid(x) for x in z_old]
            lkb = [jnp.concatenate([jnp.where(earlier[:kb], ls_new[c] - z_new[c], 0.0).astype(BF16),
                                    jnp.where(earlier, ls_old[c] - z_old[c], 0.0).astype(BF16)], axis=0)
                   for c in range(len(chains))]
            sums = [suffix_sums(x) for x in lkb]
            for c, (e, g) in enumerate(chains):
                w_new = jnp.exp(ls_new[c] + sums[c][:kb, :2 * kb])
                w_new = jnp.where(earlier[:kb], w_new, 0.0).astype(BF16)
                later = jnp.concatenate([zero, sums[c][:kb, 2 * kb:]], axis=0)
                w_old = jnp.exp(ls_old[c] + sums[c][kb:, :2 * kb] + later)
                w_old = jnp.where(earlier, w_old, 0.0).astype(BF16)
                later_scr[e, g] = later + sums[c][kb:, 2 * kb:]
                acc_new = _dot(w_new, v2_scr[g, first[e] + 1])
                acc_scr[e, :, tile(g)] = _dot(w_old, v2_scr[g, first[e]]) + jnp.concatenate(
                    [jnp.zeros((kb, LANES), F32), acc_new], axis=0)

        if n_sub == 2:
            attend_two()
        else:
            later_scr[...] = jnp.zeros_like(later_scr)
            acc_scr[...] = jnp.zeros_like(acc_scr)
            for d in reversed(range(n_sub)):
                attend(d * KEY_BLOCK)

        def pending(e, s):
            return first[e] - 1 - 2 * s >= 0

        def reach(e, s):
            top = functools.reduce(jnp.maximum, [jnp.max(later_scr[e, g]) for g in pairs])
            return jnp.where(pending(e, s), top, lowest)

        def more(c):
            return c[1] > EXP_ZERO

        def older(c):
            s = c[0]
            j = [jnp.maximum(first[e] - 1 - 2 * s, 1) for e in range(nq)]
            gone = [jnp.where(pending(e, s), 0.0, lowest) for e in range(nq)]
            z = [_dot(q2_scr[pl.ds(q0[e], qb), tile(g)], k2_scr[g, j[e] - d]) for e, g in chains for d in range(2)]
            ls = [_log_sigmoid(x) for x in z]
            lkb = [(a - x).astype(BF16) for a, x in zip(ls, z)]
            sums = [suffix_sums(jnp.concatenate(lkb[2 * c:2 * c + 2], axis=0)) for c in range(len(chains))]
            for c, (e, g) in enumerate(chains):
                later = [later_scr[e, g] + gone[e]]
                w = []
                for d in range(2):
                    rows = slice(d * qb, (d + 1) * qb)
                    w.append(jnp.exp(ls[2 * c + d] + sums[c][rows, :2 * KEY_BLOCK] + later[d]).astype(BF16))
                    later.append(later[d] + sums[c][rows, 2 * KEY_BLOCK:])
                later_scr[e, g] = later[2]
                v_two = v2_scr[g, pl.ds(j[e] - 1, 2)].reshape(4 * KEY_BLOCK, LANES)
                acc_scr[e, :, tile(g)] = acc_scr[e, :, tile(g)] + _dot(jnp.concatenate([w[1], w[0]], axis=1), v_two)
            return s + 1, functools.reduce(jnp.maximum, [reach(e, s + 1) for e in range(nq)])

        lax.while_loop(more, older, older((0, None)))
        for e in range(nq):
            o_ref[pl.ds(q0[e], qb), :] = acc_scr[e]
        return carry

    lax.fori_loop(0, t_q // (qb * nq), q_group, 0)


def _cumsum_matrix():
    s = jnp.arange(KEY_BLOCK)
    after = (s[:, None] > s[None, :]).astype(F32)
    ones = jnp.ones((KEY_BLOCK, KEY_BLOCK), F32)
    z = jnp.zeros((KEY_BLOCK, KEY_BLOCK), F32)
    bd = lambda m: jnp.concatenate([jnp.concatenate([m, z], 1), jnp.concatenate([z, m], 1)], 0)
    return jnp.concatenate([bd(after), bd(ones)], axis=1).astype(BF16)


def _sb(qkvg, k_past, v_past, batch, t_q):
    n = batch * t_q
    n_past = 0 if k_past is None else k_past.shape[3] // KEY_BLOCK
    qb = min(t_q, SB_Q_BLOCK)
    n_kb = n_past + pl.cdiv(t_q, KEY_BLOCK)
    assert n_past % 2 == 0 and (t_q <= KEY_BLOCK or qb % (2 * KEY_BLOCK) == 0), (n_past, t_q, qb)
    npair = N_PAIRS if t_q <= KEY_BLOCK else 1
    nq = 2 if (t_q // qb) % 2 == 0 else 1
    steps = N_PAIRS // npair
    width = npair * LANES
    col = lambda section: pl.BlockSpec((t_q, width), lambda b, p: (b, section * steps + p))
    in_specs = [col(0), col(1), col(2)]
    args = [qkvg, qkvg, qkvg]
    if n_past:
        past = k_past.shape[3]
        cache = pl.BlockSpec((1, 2 * npair, D_HEAD, past), lambda b, p: (b, p, 0, 0))
        in_specs += [cache, cache]
        args += [k_past, v_past]
    in_specs.append(pl.BlockSpec((2 * KEY_BLOCK, 4 * KEY_BLOCK), lambda b, p: (0, 0)))
    args.append(_cumsum_matrix())
    kv_out = pl.BlockSpec((1, 2 * npair, D_HEAD, t_q), lambda b, p: (b, p, 0, 0))
    return pl.pallas_call(
        functools.partial(_sb_kernel, t_q=t_q, qb=qb, n_past=n_past, npair=npair, nq=nq),
        grid=(batch, steps),
        in_specs=in_specs,
        out_specs=[pl.BlockSpec((t_q, width), lambda b, p: (b, p)), kv_out, kv_out],
        out_shape=[jax.ShapeDtypeStruct((n, D_MODEL), F32),
                   jax.ShapeDtypeStruct((batch, N_HEADS, D_HEAD, t_q), F32),
                   jax.ShapeDtypeStruct((batch, N_HEADS, D_HEAD, t_q), F32)],
        scratch_shapes=[pltpu.VMEM((t_q, width), BF16),
                        pltpu.VMEM((npair, n_kb, LANES, 2 * KEY_BLOCK), BF16),
                        pltpu.VMEM((npair, n_kb, 2 * KEY_BLOCK, LANES), BF16),
                        pltpu.VMEM((nq, npair, qb, 2 * KEY_BLOCK), F32),
                        pltpu.VMEM((nq, qb, width), F32)],
        compiler_params=pltpu.CompilerParams(
            dimension_semantics=("parallel", "parallel"), vmem_limit_bytes=VMEM_LIMIT),
        name="sb",
    )(*args)


def _rwkv_kernel(pb_ref, sh_ref, s0_ref, mu_ref, w0_ref, w2_ref, a0_ref, a2_ref, kk_ref, ka_ref,
                 rk_ref, lg_ref, lb_ref, bd_ref, tri_ref,
                 og_ref, so_ref, s_scr, prev_scr, *, gb, rows):
    ch = RWKV_CHUNK
    c = pl.program_id(1)
    bd = bd_ref[...]
    tri = tri_ref[...]

    lane = lax.broadcasted_iota(jnp.int32, (ch, LANES), 1)
    row = lax.broadcasted_iota(jnp.int32, (ch, LANES), 0)
    first_head = lane < D_HEAD
    t_in = lane % D_HEAD
    strict = t_in < row
    incl = t_in <= row
    eye2 = jnp.where(t_in == row, 1.0, 0.0)
    same_head = (lax.broadcasted_iota(jnp.int32, (LANES, LANES), 0) // D_HEAD
                 == lax.broadcasted_iota(jnp.int32, (LANES, LANES), 1) // D_HEAD)

    def blockdiag(y):
        return jnp.concatenate([jnp.where(first_head, y, 0.0), jnp.where(first_head, 0.0, y)],
                               axis=0).astype(BF16)

    def pair_dot(x, y):
        return _dot(x.astype(BF16), blockdiag(y))

    def dot_nt(x, y):
        return lax.dot_general(x, y, (((1,), (1,)), ((), ())), preferred_element_type=F32)

    @pl.when(c == 0)
    def _():
        zero = jnp.zeros((D_HEAD, D_HEAD), F32)
        for b in range(gb):
            prev_scr[b] = sh_ref[b]
            for j in range(N_PAIRS):
                s_scr[b, j] = jnp.concatenate(
                    [jnp.concatenate([s0_ref[b, 2 * j], zero], axis=1),
                     jnp.concatenate([zero, s0_ref[b, 2 * j + 1]], axis=1)], axis=0)

    wide_row = lax.broadcasted_iota(jnp.int32, (ch, C_SHIFT), 0)
    live = lax.broadcasted_iota(jnp.int32, (ch, D_MODEL), 0) < rows
    tok = []
    for b in range(gb):
        p = pb_ref[b]
        if rows < ch:
            p = jnp.concatenate([p, jnp.zeros((ch - rows, C_SHIFT), F32)], axis=0)
        p_prev = jnp.where(wide_row == 0, prev_scr[b], pltpu.roll(p, 1, 0))
        prev_scr[b] = p[rows - 1:rows, :]
        pm = p + mu_ref[...] * (p_prev - p)
        r = pm[:, 0:D_MODEL]
        kb = pm[:, D_MODEL:2 * D_MODEL]
        vb = pm[:, 2 * D_MODEL:3 * D_MODEL]
        lora = pm[:, LORA_OFF:LORA_OFF + LANES]
        zb = pm[:, ZB_OFF:ZB_OFF + D_MODEL]
        lw = -DECAY_SCALE * _sigmoid(w0_ref[...] + _dot(jnp.tanh(lora).astype(BF16), w2_ref[...]))
        a = _sigmoid(a0_ref[...] + _dot(lora.astype(BF16), a2_ref[...]))
        kk = kb * kk_ref[...]
        kk = kk * lax.rsqrt(jnp.maximum(_head_sum_wide(kk * kk, bd), 1e-24))
        k2 = kb * (1.0 + (a - 1.0) * ka_ref[...])
        if rows < ch:
            r, kk, k2, vb, lw = (jnp.where(live, x, 0.0) for x in (r, kk, k2, vb, lw))
        l1 = lw.astype(BF16)
        rem = lw - l1.astype(F32)
        l2 = rem.astype(BF16)
        l3 = (rem - l2.astype(F32)).astype(BF16)
        tok.append(dict(r=r, kk=kk, k2=k2, vb=vb, lw=lw, be=kk * a, zb=zb,
                        bonus=_head_sum_wide(r * k2 * rk_ref[...], bd) * vb,
                        cs=_dot(tri, l1) + _dot(tri, l2) + _dot(tri, l3)))

    chains = [(b, j) for b in range(gb) for j in range(N_PAIRS)]
    pairs = range(len(chains))
    tile = lambda name, q: tok[chains[q][0]][name][:, chains[q][1] * LANES:(chains[q][1] + 1) * LANES]
    cs_end = [tile("cs", q)[ch - 1:ch, :] for q in pairs]
    g_inv = [jnp.exp(-tile("cs", q)) for q in pairs]
    x = [jnp.concatenate([tile("kk", q) * jnp.exp(tile("cs", q) - tile("lw", q)),
                          tile("r", q) * jnp.exp(tile("cs", q))], axis=0).astype(BF16) for q in pairs]
    gram = [dot_nt(x[q], jnp.concatenate([blockdiag(tile("be", q) * g_inv[q]),
                                          blockdiag(tile("k2", q) * g_inv[q])], axis=0)) for q in pairs]
    l_ab = [jnp.where(strict, gram[q][:ch, :LANES], 0.0) for q in pairs]
    m_ak = [jnp.where(strict, gram[q][:ch, LANES:], 0.0).astype(BF16) for q in pairs]
    m_r = [jnp.concatenate([jnp.where(incl, gram[q][ch:, :LANES], 0.0),
                            jnp.where(incl, gram[q][ch:, LANES:], 0.0)], axis=1).astype(BF16)
           for q in pairs]
    inv = [eye2 - l_ab[q] for q in pairs]
    pw = [pair_dot(l_ab[q], l_ab[q]) for q in pairs]
    n_prod = ch.bit_length() - 2
    for it in range(n_prod):
        if it < n_prod - 1:
            both = [pair_dot(jnp.concatenate([inv[q], pw[q]], axis=0), pw[q]) for q in pairs]
            inv = [inv[q] + both[q][:ch] for q in pairs]
            pw = [both[q][ch:] for q in pairs]
        else:
            inv = [inv[q] + pair_dot(inv[q], pw[q]) for q in pairs]
    s_bd = [s_scr[chains[q]] for q in pairs]
    w1 = [dot_nt(x[q], s_bd[q].astype(BF16)) for q in pairs]
    v_bd = [blockdiag(tile("vb", q)) for q in pairs]
    z = [w1[q][:ch] + _dot(m_ak[q], v_bd[q]) for q in pairs]
    u = [-pair_dot(inv[q], z[q]) for q in pairs]
    outs = [w1[q][ch:] + _dot(m_r[q], jnp.concatenate([blockdiag(u[q]), v_bd[q]], axis=0))
            for q in pairs]
    for q in pairs:
        g_end = jnp.exp(cs_end[q] - tile("cs", q))
        uv_t = jnp.concatenate([u[q], tile("vb", q)], axis=0).T.astype(BF16)
        bk = jnp.concatenate([tile("be", q) * g_end, tile("k2", q) * g_end], axis=0).astype(BF16)
        s_scr[chains[q]] = s_bd[q] * jnp.exp(cs_end[q]) + jnp.where(same_head, _dot(uv_t, bk), 0.0)

    for b in range(gb):
        o = jnp.concatenate(outs[b * N_PAIRS:(b + 1) * N_PAIRS], axis=1)
        mean = _head_sum_wide(o, bd) * (1.0 / D_HEAD)
        dlt = o - mean
        var = _head_sum_wide(dlt * dlt, bd) * (1.0 / D_HEAD)
        o = dlt * lax.rsqrt(var + LNX_EPS) * lg_ref[...] + lb_ref[...] + tok[b]["bonus"]
        zb = tok[b]["zb"]
        og_ref[b] = (o * (zb * _sigmoid(zb)))[:rows].astype(BF16)

    @pl.when(c == pl.num_programs(1) - 1)
    def _():
        for b in range(gb):
            for j in range(N_PAIRS):
                so_ref[b, 2 * j] = s_scr[b, j, :D_HEAD, :D_HEAD]
                so_ref[b, 2 * j + 1] = s_scr[b, j, D_HEAD:, D_HEAD:]


def _rwkv(pb, shift_prev, wkv_prev, prm, gb):
    batch, t, _ = pb.shape
    rows = min(t, RWKV_CHUNK)
    row = lambda v: v.reshape(1, -1)
    const = lambda shape: pl.BlockSpec(shape, lambda g, c: (0,) * len(shape))
    zeros = jnp.zeros((D_HEAD, D_MODEL), F32)
    w2p = jnp.concatenate([prm["w2"], zeros], axis=0).astype(BF16)
    a2p = jnp.concatenate([zeros, prm["a2"]], axis=0).astype(BF16)
    head = jnp.arange(LANES) // D_HEAD
    bd = (head[:, None] == head[None, :]).astype(BF16)
    tok = jnp.arange(RWKV_CHUNK)
    tri = (tok[:, None] >= tok[None, :]).astype(BF16)
    vec_spec = const((1, D_MODEL))
    return pl.pallas_call(
        functools.partial(_rwkv_kernel, gb=gb, rows=rows),
        grid=(batch // gb, t // rows),
        in_specs=[pl.BlockSpec((gb, rows, C_SHIFT), lambda g, c: (g, c, 0)),
                  pl.BlockSpec((gb, 1, C_SHIFT), lambda g, c: (g, 0, 0)),
                  pl.BlockSpec((gb, N_HEADS, D_HEAD, D_HEAD), lambda g, c: (g, 0, 0, 0)),
                  const((1, C_SHIFT)), vec_spec, const((LANES, D_MODEL)), vec_spec,
                  const((LANES, D_MODEL)), vec_spec, vec_spec, vec_spec, vec_spec, vec_spec,
                  const((LANES, LANES)), const((RWKV_CHUNK, RWKV_CHUNK))],
        out_specs=[pl.BlockSpec((gb, rows, D_MODEL), lambda g, c: (g, c, 0)),
                   pl.BlockSpec((gb, N_HEADS, D_HEAD, D_HEAD), lambda g, c: (g, 0, 0, 0))],
        out_shape=[jax.ShapeDtypeStruct((batch, t, D_MODEL), BF16),
                   jax.ShapeDtypeStruct((batch, N_HEADS, D_HEAD, D_HEAD), F32)],
        scratch_shapes=[pltpu.VMEM((gb, N_PAIRS, LANES, LANES), F32),
                        pltpu.VMEM((gb, 1, C_SHIFT), F32)],
        compiler_params=pltpu.CompilerParams(
            dimension_semantics=("parallel", "arbitrary"), vmem_limit_bytes=VMEM_LIMIT),
        name="rwkv",
    )(pb, shift_prev, wkv_prev, row(prm["mu_shift"]), row(prm["w0"]), w2p, row(prm["a0"]), a2p,
      row(prm["k_k"]), row(prm["k_a"]), row(prm["r_k"]), row(prm["lnx_g"]), row(prm["lnx_b"]), bd, tri)


def _out_kernel(x_ref, za_ref, ga_ref, gb_ref, oa_ref, ob_ref, woa_ref, wob_ref, wout_ref, fg_ref,
                y_ref, *, final_norm):
    za = za_ref[...]
    y_a = _dot((oa_ref[...] * (za * _sigmoid(za))).astype(BF16), woa_ref[...])
    y_b = _dot(ob_ref[...], wob_ref[...])
    merged = _sigmoid(ga_ref[...]) * y_a + _sigmoid(gb_ref[...]) * y_b
    y = x_ref[...] + _dot(merged.astype(BF16), wout_ref[...])
    if final_norm:
        ms = jnp.mean(y * y, axis=-1, keepdims=True)
        y = y * lax.rsqrt(ms + EPS) * fg_ref[...]
    y_ref[...] = y


def _out(x2, qkvg, o_a, og_b, w_o_a, w_o_b, w_out, final_g, tm, final_norm):
    n, d = x2.shape
    tok = lambda blk: pl.BlockSpec((tm, d), lambda i: (i, blk))
    wspec = pl.BlockSpec((d, d), lambda i: (0, 0))
    return pl.pallas_call(
        functools.partial(_out_kernel, final_norm=final_norm),
        grid=(n // tm,),
        in_specs=[tok(0), tok(3), tok(4), tok(5), tok(0), tok(0), wspec, wspec, wspec,
                  pl.BlockSpec((1, d), lambda i: (0, 0))],
        out_specs=tok(0),
        out_shape=jax.ShapeDtypeStruct((n, d), F32),
        compiler_params=pltpu.CompilerParams(
            dimension_semantics=("parallel",), vmem_limit_bytes=VMEM_LIMIT),
        name="out",
    )(x2, qkvg, qkvg, qkvg, o_a, og_b, w_o_a, w_o_b, w_out, final_g)


def _layer(x, shift_prev, k_past, v_past, wkv_prev, prm, final_g, final_norm):
    batch, t, d = x.shape
    n = batch * t
    x2 = x.reshape(n, d)
    g = prm["norm_g"].reshape(1, d)
    tm = min(n, PROJ_ROWS)
    qkvg = _proj(x2, g, prm["w_in_a"], tm, 3 * D_MODEL)
    pb = _proj(x2, g, prm["w_in_b"], min(n, 512), C_SHIFT).reshape(batch, t, C_SHIFT)
    dim_major = lambda c: None if c is None else jnp.swapaxes(c, -1, -2)
    o_a, k_new, v_new = _sb(qkvg, dim_major(k_past), dim_major(v_past), batch, t)
    og_b, wkv_new = _rwkv(pb, shift_prev, wkv_prev, prm, RWKV_GROUP)
    y = _out(x2, qkvg, o_a, og_b.reshape(n, d), prm["w_o_a"], prm["w_o_b"], prm["w_out"],
             final_g.reshape(1, d), min(n, OUT_ROWS), final_norm)
    return (y.reshape(batch, t, d), jnp.swapaxes(k_new, -1, -2), jnp.swapaxes(v_new, -1, -2),
            pb[:, t - 1:, :], wkv_new)


def kernel(x_prompt, x_sample, cache_sb_k, cache_sb_v, state_shift, state_wkv, norm_g, w_in, mu_shift,
           w0, w2, a0, a2, k_k, k_a, r_k, lnx_g, lnx_b, w_o_a, w_o_b, w_out, final_norm_g):
    depth = w_in.shape[0]
    batch = x_prompt.shape[0]
    xp, xs = x_prompt, x_sample
    outs_p, outs_s = [], []
    for l in range(depth):
        prm = dict(norm_g=norm_g[l],
                   w_in_a=w_in[l][:, :C_QKVG].astype(BF16), w_in_b=w_in[l][:, C_QKVG:].astype(BF16),
                   mu_shift=mu_shift[l], w0=w0[l], w2=w2[l], a0=a0[l], a2=a2[l], k_k=k_k[l], k_a=k_a[l],
                   r_k=r_k[l].reshape(-1), lnx_g=lnx_g[l], lnx_b=lnx_b[l],
                   w_o_a=w_o_a[l].astype(BF16), w_o_b=w_o_b[l].astype(BF16), w_out=w_out[l].astype(BF16))
        last = l == depth - 1
        zero_shift = jnp.zeros((batch, 1, C_SHIFT), F32)
        zero_wkv = jnp.zeros((batch, N_HEADS, D_HEAD, D_HEAD), F32)
        xp, *rest_p = _layer(xp, zero_shift, None, None, zero_wkv, prm, final_norm_g, last)
        xs, *rest_s = _layer(xs, state_shift[l], cache_sb_k[l], cache_sb_v[l], state_wkv[l],
                             prm, final_norm_g, last)
        outs_p.append(rest_p)
        outs_s.append(rest_s)
    stack = lambda outs, i: jnp.stack([o[i] for o in outs])
    return (xp, xs,
            stack(outs_p, 0), stack(outs_p, 1), stack(outs_p, 2), stack(outs_p, 3),
            stack(outs_s, 0), stack(outs_s, 1), stack(outs_s, 2), stack(outs_s, 3))
```

```python
import functools

import jax
import jax.numpy as jnp
from jax import lax
from jax.experimental import pallas as pl
from jax.experimental.pallas import tpu as pltpu

F32 = jnp.float32
BF16 = jnp.bfloat16

D_MODEL = 1024
N_HEADS = 16
D_HEAD = 64
LANES = 128
SUBLANES = 8
N_PAIRS = N_HEADS // 2
C_QKVG = 6 * D_MODEL
C_SHIFT = 3 * D_MODEL + 2 * D_HEAD + D_MODEL
LORA_OFF = 3 * D_MODEL
ZB_OFF = LORA_OFF + LANES
EPS = 1e-6
LNX_EPS = 64e-5
DECAY_SCALE = 0.6065306597126334
PROJ_ROWS = 1024
OUT_ROWS = 512
KEY_BLOCK = 128
SB_Q_BLOCK = 256
EXP_ZERO = -110.0
RWKV_GROUP = 4
RWKV_CHUNK = 64
VMEM_LIMIT = 56 * 1024 * 1024


def _log_sigmoid(x):
    return jnp.minimum(x, 0.0) - jnp.log(1.0 + jnp.exp(-jnp.abs(x)))


def _sigmoid(x):
    return 1.0 / (1.0 + jnp.exp(-x))


def _dot(a, b):
    return jnp.dot(a, b, preferred_element_type=F32)


def _head_sum(x, bd):
    return _dot(x.astype(BF16), bd)


def _head_sum_wide(x, bd):
    return jnp.concatenate(
        [_head_sum(x[:, j * LANES:(j + 1) * LANES], bd) for j in range(x.shape[1] // LANES)], axis=1)


def _proj_kernel(x_ref, g_ref, w_ref, o_ref, h_scr):
    @pl.when(pl.program_id(1) == 0)
    def _():
        x = x_ref[...]
        ms = jnp.mean(x * x, axis=-1, keepdims=True)
        h_scr[...] = (x * lax.rsqrt(ms + EPS) * g_ref[...]).astype(BF16)

    o_ref[...] = _dot(h_scr[...], w_ref[...])


def _proj(x2, g, w, tm, tn):
    n, d = x2.shape
    n_out = w.shape[1]
    return pl.pallas_call(
        _proj_kernel,
        grid=(n // tm, n_out // tn),
        in_specs=[pl.BlockSpec((tm, d), lambda i, j: (i, 0)),
                  pl.BlockSpec((1, d), lambda i, j: (0, 0)),
                  pl.BlockSpec((d, tn), lambda i, j: (0, j))],
        out_specs=pl.BlockSpec((tm, tn), lambda i, j: (i, j)),
        out_shape=jax.ShapeDtypeStruct((n, n_out), F32),
        scratch_shapes=[pltpu.VMEM((tm, d), BF16)],
        compiler_params=pltpu.CompilerParams(
            dimension_semantics=("parallel", "arbitrary"), vmem_limit_bytes=VMEM_LIMIT),
        name="proj",
    )(x2, g, w)


def _sb_kernel(*refs, t_q, qb, n_past, npair, nq):
    if n_past:
        (q_ref, k_ref, v_ref, kc_hbm, vc_hbm, mc_ref,
         o_ref, ko_ref, vo_ref, q2_scr, k2_scr, v2_scr, later_scr, acc_scr, kbuf, vbuf, sem) = refs
    else:
        (q_ref, k_ref, v_ref, mc_ref,
         o_ref, ko_ref, vo_ref, q2_scr, k2_scr, v2_scr, later_scr, acc_scr) = refs
    pairs = range(npair)
    tile = lambda g: slice(g * LANES, (g + 1) * LANES)
    batch_row = pl.program_id(0)

    def cache_copies(slot, j):
        c0 = pl.multiple_of((j - 1) * KEY_BLOCK, KEY_BLOCK)
        return [pltpu.make_async_copy(src.at[batch_row, :, :, pl.ds(c0, 2 * KEY_BLOCK)], buf.at[slot],
                                      sem.at[i, slot])
                for i, (src, buf) in enumerate(((kc_hbm, kbuf), (vc_hbm, vbuf)))]

    if n_past:
        for cp in cache_copies(0, n_past - 1):
            cp.start()

    lane = lax.broadcasted_iota(jnp.int32, (KEY_BLOCK, LANES), 1)
    sub = lax.broadcasted_iota(jnp.int32, (LANES, KEY_BLOCK), 0)
    first_head = lane < D_HEAD
    first_head_rows = sub < D_HEAD

    q2_scr[...] = (q_ref[...] * (D_HEAD ** -0.5)).astype(BF16)
    for g in pairs:
        kp = k_ref[:, tile(g)]
        vp = v_ref[:, tile(g)]
        if t_q < KEY_BLOCK:
            pad = jnp.zeros((KEY_BLOCK - t_q, LANES), F32)
            kp = jnp.concatenate([kp, pad], axis=0)
            vp = jnp.concatenate([vp, pad], axis=0)

        kpt = kp.T
        vpt = vp.T
        for h in range(2):
            ko_ref[0, 2 * g + h] = kpt[h * D_HEAD:(h + 1) * D_HEAD, :t_q]
            vo_ref[0, 2 * g + h] = vpt[h * D_HEAD:(h + 1) * D_HEAD, :t_q]

        def put_block(j, ktj, vj):
            k2_scr[g, j, :, :KEY_BLOCK] = jnp.where(first_head_rows, ktj, 0.0).astype(BF16)
            k2_scr[g, j, :, KEY_BLOCK:] = jnp.where(first_head_rows, 0.0, ktj).astype(BF16)
            v2_scr[g, j, :KEY_BLOCK] = jnp.where(first_head, vj, 0.0).astype(BF16)
            v2_scr[g, j, KEY_BLOCK:] = jnp.where(first_head, 0.0, vj).astype(BF16)

        for j in range(kp.shape[0] // KEY_BLOCK):
            cols = slice(j * KEY_BLOCK, (j + 1) * KEY_BLOCK)
            put_block(j, kpt[:, cols], vp[cols])

    row = lax.broadcasted_iota(jnp.int32, (qb, 2 * KEY_BLOCK), 0)
    col = lax.broadcasted_iota(jnp.int32, (qb, 2 * KEY_BLOCK), 1) % KEY_BLOCK
    earlier = col < row
    mc = mc_ref[...]

    def suffix_sums(lkb):
        return _dot(lkb, mc)
    n_sub = max(qb // KEY_BLOCK, 1)

    lowest = jnp.float32(-3.0e38)
    chains = [(e, g) for e in range(nq) for g in pairs]

    def q_group(u, carry):
        q0 = [pl.multiple_of((u * nq + e) * qb, qb) for e in range(nq)]
        first = [n_past + (u * nq + e) * n_sub for e in range(nq)]

        def attend(r0):
            m = qb - r0
            d = r0 // KEY_BLOCK
            z = [_dot(q2_scr[pl.ds(q0[e] + r0, m), tile(g)], k2_scr[g, first[e] + d - n_past]) for e, g in chains]
            ls = [_log_sigmoid(x) for x in z]
            lkb = [jnp.where(earlier[:m], a - x, 0.0).astype(BF16) for a, x in zip(ls, z)]
            sums = [suffix_sums(x) for x in lkb]
            for c, (e, g) in enumerate(chains):
                later = later_scr[e, g, r0:, :]
                w = jnp.exp(ls[c] + sums[c][:, :2 * KEY_BLOCK] + later)
                w = jnp.where(earlier[:m], w, 0.0).astype(BF16)
                later_scr[e, g, r0:, :] = later + sums[c][:, 2 * KEY_BLOCK:]
                acc_scr[e, r0:, tile(g)] = acc_scr[e, r0:, tile(g)] + _dot(w, v2_scr[g, first[e] + d - n_past])

        def attend_two():
            kb = KEY_BLOCK
            zero = jnp.zeros((kb, 2 * kb), F32)
            z_new = [_dot(q2_scr[pl.ds(q0[e] + kb, kb), tile(g)], k2_scr[g, first[e] + 1]) for e, g in chains]
            z_old = [_dot(q2_scr[pl.ds(q0[e], qb), tile(g)], k2_scr[g, first[e]]) for e, g in chains]
            ls_new = [_log_sigmoid(x) for x in z_new]
            ls_old = [_log_sigmoid(x) for x in z_old]
            lkb = [jnp.concatenate([jnp.where(earlier[:kb], ls_new[c] - z_new[c], 0.0).astype(BF16),
                                    jnp.where(earlier, ls_old[c] - z_old[c], 0.0).astype(BF16)], axis=0)
                   for c in range(len(chains))]
            sums = [suffix_sums(x) for x in lkb]
            for c, (e, g) in enumerate(chains):
                w_new = jnp.exp(ls_new[c] + sums[c][:kb, :2 * kb])
                w_new = jnp.where(earlier[:kb], w_new, 0.0).astype(BF16)
                later = jnp.concatenate([zero, sums[c][:kb, 2 * kb:]], axis=0)
                w_old = jnp.exp(ls_old[c] + sums[c][kb:, :2 * kb] + later)
                w_old = jnp.where(earlier, w_old, 0.0).astype(BF16)
                later_scr[e, g] = later + sums[c][kb:, 2 * kb:]
                acc_new = _dot(w_new, v2_scr[g, first[e] + 1])
                acc_scr[e, :, tile(g)] = _dot(w_old, v2_scr[g, first[e]]) + jnp.concatenate(
                    [jnp.zeros((kb, LANES), F32), acc_new], axis=0)

        if n_sub == 2:
            attend_two()
        else:
            later_scr[...] = jnp.zeros_like(later_scr)
            acc_scr[...] = jnp.zeros_like(acc_scr)
            for d in reversed(range(n_sub)):
                attend(d * KEY_BLOCK)

        def pending(e, s):
            return first[e] - 1 - 2 * s >= 0

        def reach(e, s):
            top = functools.reduce(jnp.maximum, [jnp.max(later_scr[e, g]) for g in pairs])
            return jnp.where(pending(e, s), top, lowest)

        def more(c):
            return c[1] > EXP_ZERO

        def older(c):
            s = c[0]
            j = [jnp.maximum(first[e] - 1 - 2 * s, 1) for e in range(nq)]
            gone = [jnp.where(pending(e, s), 0.0, lowest) for e in range(nq)]
            if n_past:
                slot = s % 2
                for cp in cache_copies(slot, 1):
                    cp.wait()

                @pl.when(j[0] - 2 >= 1)
                def _():
                    for cp in cache_copies(1 - slot, j[0] - 2):
                        cp.start()

                def from_hbm(buf, g, d):
                    cols = slice((1 - d) * KEY_BLOCK, (2 - d) * KEY_BLOCK)
                    return jnp.concatenate([buf[slot, 2 * g, :, cols], buf[slot, 2 * g + 1, :, cols]], axis=0)

                def k_block(e, g, d):
                    kt = from_hbm(kbuf, g, d)
                    return jnp.concatenate([jnp.where(first_head_rows, kt, 0.0),
                                            jnp.where(first_head_rows, 0.0, kt)], axis=1).astype(BF16)

                def v_blocks(e, g):
                    vs = [from_hbm(vbuf, g, d).T for d in (1, 0)]
                    return jnp.concatenate([jnp.where(first_head, v, 0.0) if h == 0 else jnp.where(first_head, 0.0, v)
                                            for v in vs for h in range(2)], axis=0).astype(BF16)
            else:
                k_block = lambda e, g, d: k2_scr[g, j[e] - d]
                v_blocks = lambda e, g: v2_scr[g, pl.ds(j[e] - 1, 2)].reshape(4 * KEY_BLOCK, LANES)
            z = [_dot(q2_scr[pl.ds(q0[e], qb), tile(g)], k_block(e, g, d)) for e, g in chains for d in range(2)]
            ls = [_log_sigmoid(x) for x in z]
            lkb = [(a - x).astype(BF16) for a, x in zip(ls, z)]
            sums = [suffix_sums(jnp.concatenate(lkb[2 * c:2 * c + 2], axis=0)) for c in range(len(chains))]
            for c, (e, g) in enumerate(chains):
                later = [later_scr[e, g] + gone[e]]
                w = []
                for d in range(2):
                    rows = slice(d * qb, (d + 1) * qb)
                    w.append(jnp.exp(ls[2 * c + d] + sums[c][rows, :2 * KEY_BLOCK] + later[d]).astype(BF16))
                    later.append(later[d] + sums[c][rows, 2 * KEY_BLOCK:])
                later_scr[e, g] = later[2]
                acc_scr[e, :, tile(g)] = acc_scr[e, :, tile(g)] + _dot(
                    jnp.concatenate([w[1], w[0]], axis=1), v_blocks(e, g))
            return s + 1, functools.reduce(jnp.maximum, [reach(e, s + 1) for e in range(nq)])

        steps_done, _ = lax.while_loop(more, older, older((0, None)))
        if n_past:
            @pl.when(first[0] - 1 - 2 * steps_done >= 1)
            def _():
                for cp in cache_copies(steps_done % 2, 1):
                    cp.wait()
        for e in range(nq):
            o_ref[pl.ds(q0[e], qb), :] = acc_scr[e]
        return carry

    lax.fori_loop(0, t_q // (qb * nq), q_group, 0)


def _cumsum_matrix():
    s = jnp.arange(KEY_BLOCK)
    after = (s[:, None] > s[None, :]).astype(F32)
    ones = jnp.ones((KEY_BLOCK, KEY_BLOCK), F32)
    z = jnp.zeros((KEY_BLOCK, KEY_BLOCK), F32)
    bd = lambda m: jnp.concatenate([jnp.concatenate([m, z], 1), jnp.concatenate([z, m], 1)], 0)
    return jnp.concatenate([bd(after), bd(ones)], axis=1).astype(BF16)


def _sb(qkvg, k_past, v_past, batch, t_q):
    n = batch * t_q
    n_past = 0 if k_past is None else k_past.shape[3] // KEY_BLOCK
    qb = min(t_q, SB_Q_BLOCK)
    n_kb = pl.cdiv(t_q, KEY_BLOCK)
    assert n_past % 2 == 0 and (t_q <= KEY_BLOCK or qb % (2 * KEY_BLOCK) == 0), (n_past, t_q, qb)
    npair = N_PAIRS if t_q <= KEY_BLOCK else 1
    nq = 2 if (t_q // qb) % 2 == 0 else 1
    steps = N_PAIRS // npair
    width = npair * LANES
    col = lambda section: pl.BlockSpec((t_q, width), lambda b, p: (b, section * steps + p))
    in_specs = [col(0), col(1), col(2)]
    args = [qkvg, qkvg, qkvg]
    if n_past:
        assert npair == N_PAIRS
        in_specs += [pl.BlockSpec(memory_space=pl.ANY)] * 2
        args += [k_past, v_past]
    in_specs.append(pl.BlockSpec((2 * KEY_BLOCK, 4 * KEY_BLOCK), lambda b, p: (0, 0)))
    args.append(_cumsum_matrix())
    kv_out = pl.BlockSpec((1, 2 * npair, D_HEAD, t_q), lambda b, p: (b, p, 0, 0))
    past_scratch = [pltpu.VMEM((2, N_HEADS, D_HEAD, 2 * KEY_BLOCK), F32)] * 2 + [
        pltpu.SemaphoreType.DMA((2, 2))] if n_past else []
    return pl.pallas_call(
        functools.partial(_sb_kernel, t_q=t_q, qb=qb, n_past=n_past, npair=npair, nq=nq),
        grid=(batch, steps),
        in_specs=in_specs,
        out_specs=[pl.BlockSpec((t_q, width), lambda b, p: (b, p)), kv_out, kv_out],
        out_shape=[jax.ShapeDtypeStruct((n, D_MODEL), F32),
                   jax.ShapeDtypeStruct((batch, N_HEADS, D_HEAD, t_q), F32),
                   jax.ShapeDtypeStruct((batch, N_HEADS, D_HEAD, t_q), F32)],
        scratch_shapes=[pltpu.VMEM((t_q, width), BF16),
                        pltpu.VMEM((npair, n_kb, LANES, 2 * KEY_BLOCK), BF16),
                        pltpu.VMEM((npair, n_kb, 2 * KEY_BLOCK, LANES), BF16),
                        pltpu.VMEM((nq, npair, qb, 2 * KEY_BLOCK), F32),
                        pltpu.VMEM((nq, qb, width), F32)] + past_scratch,
        compiler_params=pltpu.CompilerParams(
            dimension_semantics=("parallel", "parallel"), vmem_limit_bytes=VMEM_LIMIT),
        name="sb",
    )(*args)


def _rwkv_kernel(pb_ref, sh_ref, s0_ref, mu_ref, w0_ref, w2_ref, a0_ref, a2_ref, kk_ref, ka_ref,
                 rk_ref, lg_ref, lb_ref, bd_ref, tri_ref,
                 og_ref, so_ref, s_scr, prev_scr, *, gb, rows):
    ch = RWKV_CHUNK
    c = pl.program_id(1)
    bd = bd_ref[...]
    tri = tri_ref[...]

    lane = lax.broadcasted_iota(jnp.int32, (ch, LANES), 1)
    row = lax.broadcasted_iota(jnp.int32, (ch, LANES), 0)
    first_head = lane < D_HEAD
    t_in = lane % D_HEAD
    strict = t_in < row
    incl = t_in <= row
    eye2 = jnp.where(t_in == row, 1.0, 0.0)
    same_head = (lax.broadcasted_iota(jnp.int32, (LANES, LANES), 0) // D_HEAD
                 == lax.broadcasted_iota(jnp.int32, (LANES, LANES), 1) // D_HEAD)

    def blockdiag(y):
        return jnp.concatenate([jnp.where(first_head, y, 0.0), jnp.where(first_head, 0.0, y)],
                               axis=0).astype(BF16)

    def pair_dot(x, y):
        return _dot(x.astype(BF16), blockdiag(y))

    def dot_nt(x, y):
        return lax.dot_general(x, y, (((1,), (1,)), ((), ())), preferred_element_type=F32)

    @pl.when(c == 0)
    def _():
        zero = jnp.zeros((D_HEAD, D_HEAD), F32)
        for b in range(gb):
            prev_scr[b] = sh_ref[b]
            for j in range(N_PAIRS):
                s_scr[b, j] = jnp.concatenate(
                    [jnp.concatenate([s0_ref[b, 2 * j], zero], axis=1),
                     jnp.concatenate([zero, s0_ref[b, 2 * j + 1]], axis=1)], axis=0)

    wide_row = lax.broadcasted_iota(jnp.int32, (ch, C_SHIFT), 0)
    live = lax.broadcasted_iota(jnp.int32, (ch, D_MODEL), 0) < rows
    tok = []
    for b in range(gb):
        p = pb_ref[b]
        if rows < ch:
            p = jnp.concatenate([p, jnp.zeros((ch - rows, C_SHIFT), F32)], axis=0)
        p_prev = jnp.where(wide_row == 0, prev_scr[b], pltpu.roll(p, 1, 0))
        prev_scr[b] = p[rows - 1:rows, :]
        pm = p + mu_ref[...] * (p_prev - p)
        r = pm[:, 0:D_MODEL]
        kb = pm[:, D_MODEL:2 * D_MODEL]
        vb = pm[:, 2 * D_MODEL:3 * D_MODEL]
        lora = pm[:, LORA_OFF:LORA_OFF + LANES]
        zb = pm[:, ZB_OFF:ZB_OFF + D_MODEL]
        lw = -DECAY_SCALE * _sigmoid(w0_ref[...] + _dot(jnp.tanh(lora).astype(BF16), w2_ref[...]))
        a = _sigmoid(a0_ref[...] + _dot(lora.astype(BF16), a2_ref[...]))
        kk = kb * kk_ref[...]
        kk = kk * lax.rsqrt(jnp.maximum(_head_sum_wide(kk * kk, bd), 1e-24))
        k2 = kb * (1.0 + (a - 1.0) * ka_ref[...])
        if rows < ch:
            r, kk, k2, vb, lw = (jnp.where(live, x, 0.0) for x in (r, kk, k2, vb, lw))
        l1 = lw.astype(BF16)
        rem = lw - l1.astype(F32)
        l2 = rem.astype(BF16)
        l3 = (rem - l2.astype(F32)).astype(BF16)
        tok.append(dict(r=r, kk=kk, k2=k2, vb=vb, lw=lw, be=kk * a, zb=zb,
                        bonus=_head_sum_wide(r * k2 * rk_ref[...], bd) * vb,
                        cs=_dot(tri, l1) + _dot(tri, l2) + _dot(tri, l3)))

    chains = [(b, j) for b in range(gb) for j in range(N_PAIRS)]
    pairs = range(len(chains))
    tile = lambda name, q: tok[chains[q][0]][name][:, chains[q][1] * LANES:(chains[q][1] + 1) * LANES]
    cs_end = [tile("cs", q)[ch - 1:ch, :] for q in pairs]
    g_inv = [jnp.exp(-tile("cs", q)) for q in pairs]
    x = [jnp.concatenate([tile("kk", q) * jnp.exp(tile("cs", q) - tile("lw", q)),
                          tile("r", q) * jnp.exp(tile("cs", q))], axis=0).astype(BF16) for q in pairs]
    gram = [dot_nt(x[q], jnp.concatenate([blockdiag(tile("be", q) * g_inv[q]),
                                          blockdiag(tile("k2", q) * g_inv[q])], axis=0)) for q in pairs]
    l_ab = [jnp.where(strict, gram[q][:ch, :LANES], 0.0) for q in pairs]
    m_ak = [jnp.where(strict, gram[q][:ch, LANES:], 0.0).astype(BF16) for q in pairs]
    m_r = [jnp.concatenate([jnp.where(incl, gram[q][ch:, :LANES], 0.0),
                            jnp.where(incl, gram[q][ch:, LANES:], 0.0)], axis=1).astype(BF16)
           for q in pairs]
    inv = [eye2 - l_ab[q] for q in pairs]
    pw = [pair_dot(l_ab[q], l_ab[q]) for q in pairs]
    n_prod = ch.bit_length() - 2
    for it in range(n_prod):
        if it < n_prod - 1:
            both = [pair_dot(jnp.concatenate([inv[q], pw[q]], axis=0), pw[q]) for q in pairs]
            inv = [inv[q] + both[q][:ch] for q in pairs]
            pw = [both[q][ch:] for q in pairs]
        else:
            inv = [inv[q] + pair_dot(inv[q], pw[q]) for q in pairs]
    s_bd = [s_scr[chains[q]] for q in pairs]
    w1 = [dot_nt(x[q], s_bd[q].astype(BF16)) for q in pairs]
    v_bd = [blockdiag(tile("vb", q)) for q in pairs]
    z = [w1[q][:ch] + _dot(m_ak[q], v_bd[q]) for q in pairs]
    u = [-pair_dot(inv[q], z[q]) for q in pairs]
    outs = [w1[q][ch:] + _dot(m_r[q], jnp.concatenate([blockdiag(u[q]), v_bd[q]], axis=0))
            for q in pairs]
    for q in pairs:
        g_end = jnp.exp(cs_end[q] - tile("cs", q))
        uv_t = jnp.concatenate([u[q], tile("vb", q)], axis=0).T.astype(BF16)
        bk = jnp.concatenate([tile("be", q) * g_end, tile("k2", q) * g_end], axis=0).astype(BF16)
        s_scr[chains[q]] = s_bd[q] * jnp.exp(cs_end[q]) + jnp.where(same_head, _dot(uv_t, bk), 0.0)

    for b in range(gb):
        o = jnp.concatenate(outs[b * N_PAIRS:(b + 1) * N_PAIRS], axis=1)
        mean = _head_sum_wide(o, bd) * (1.0 / D_HEAD)
        dlt = o - mean
        var = _head_sum_wide(dlt * dlt, bd) * (1.0 / D_HEAD)
        o = dlt * lax.rsqrt(var + LNX_EPS) * lg_ref[...] + lb_ref[...] + tok[b]["bonus"]
        zb = tok[b]["zb"]
        og_ref[b] = (o * (zb * _sigmoid(zb)))[:rows].astype(BF16)

    @pl.when(c == pl.num_programs(1) - 1)
    def _():
        for b in range(gb):
            for j in range(N_PAIRS):
                so_ref[b, 2 * j] = s_scr[b, j, :D_HEAD, :D_HEAD]
                so_ref[b, 2 * j + 1] = s_scr[b, j, D_HEAD:, D_HEAD:]


def _rwkv(pb, shift_prev, wkv_prev, prm, gb):
    batch, t, _ = pb.shape
    rows = min(t, RWKV_CHUNK)
    row = lambda v: v.reshape(1, -1)
    const = lambda shape: pl.BlockSpec(shape, lambda g, c: (0,) * len(shape))
    zeros = jnp.zeros((D_HEAD, D_MODEL), F32)
    w2p = jnp.concatenate([prm["w2"], zeros], axis=0).astype(BF16)
    a2p = jnp.concatenate([zeros, prm["a2"]], axis=0).astype(BF16)
    head = jnp.arange(LANES) // D_HEAD
    bd = (head[:, None] == head[None, :]).astype(BF16)
    tok = jnp.arange(RWKV_CHUNK)
    tri = (tok[:, None] >= tok[None, :]).astype(BF16)
    vec_spec = const((1, D_MODEL))
    return pl.pallas_call(
        functools.partial(_rwkv_kernel, gb=gb, rows=rows),
        grid=(batch // gb, t // rows),
        in_specs=[pl.BlockSpec((gb, rows, C_SHIFT), lambda g, c: (g, c, 0)),
                  pl.BlockSpec((gb, 1, C_SHIFT), lambda g, c: (g, 0, 0)),
                  pl.BlockSpec((gb, N_HEADS, D_HEAD, D_HEAD), lambda g, c: (g, 0, 0, 0)),
                  const((1, C_SHIFT)), vec_spec, const((LANES, D_MODEL)), vec_spec,
                  const((LANES, D_MODEL)), vec_spec, vec_spec, vec_spec, vec_spec, vec_spec,
                  const((LANES, LANES)), const((RWKV_CHUNK, RWKV_CHUNK))],
        out_specs=[pl.BlockSpec((gb, rows, D_MODEL), lambda g, c: (g, c, 0)),
                   pl.BlockSpec((gb, N_HEADS, D_HEAD, D_HEAD), lambda g, c: (g, 0, 0, 0))],
        out_shape=[jax.ShapeDtypeStruct((batch, t, D_MODEL), BF16),
                   jax.ShapeDtypeStruct((batch, N_HEADS, D_HEAD, D_HEAD), F32)],
        scratch_shapes=[pltpu.VMEM((gb, N_PAIRS, LANES, LANES), F32),
                        pltpu.VMEM((gb, 1, C_SHIFT), F32)],
        compiler_params=pltpu.CompilerParams(
            dimension_semantics=("parallel", "arbitrary"), vmem_limit_bytes=VMEM_LIMIT),
        name="rwkv",
    )(pb, shift_prev, wkv_prev, row(prm["mu_shift"]), row(prm["w0"]), w2p, row(prm["a0"]), a2p,
      row(prm["k_k"]), row(prm["k_a"]), row(prm["r_k"]), row(prm["lnx_g"]), row(prm["lnx_b"]), bd, tri)


def _out_kernel(x_ref, za_ref, ga_ref, gb_ref, oa_ref, ob_ref, woa_ref, wob_ref, wout_ref, fg_ref,
                y_ref, *, final_norm):
    za = za_ref[...]
    y_a = _dot((oa_ref[...] * (za * _sigmoid(za))).astype(BF16), woa_ref[...])
    y_b = _dot(ob_ref[...], wob_ref[...])
    merged = _sigmoid(ga_ref[...]) * y_a + _sigmoid(gb_ref[...]) * y_b
    y = x_ref[...] + _dot(merged.astype(BF16), wout_ref[...])
    if final_norm:
        ms = jnp.mean(y * y, axis=-1, keepdims=True)
        y = y * lax.rsqrt(ms + EPS) * fg_ref[...]
    y_ref[...] = y


def _out(x2, qkvg, o_a, og_b, w_o_a, w_o_b, w_out, final_g, tm, final_norm):
    n, d = x2.shape
    tok = lambda blk: pl.BlockSpec((tm, d), lambda i: (i, blk))
    wspec = pl.BlockSpec((d, d), lambda i: (0, 0))
    return pl.pallas_call(
        functools.partial(_out_kernel, final_norm=final_norm),
        grid=(n // tm,),
        in_specs=[tok(0), tok(3), tok(4), tok(5), tok(0), tok(0), wspec, wspec, wspec,
                  pl.BlockSpec((1, d), lambda i: (0, 0))],
        out_specs=tok(0),
        out_shape=jax.ShapeDtypeStruct((n, d), F32),
        compiler_params=pltpu.CompilerParams(
            dimension_semantics=("parallel",), vmem_limit_bytes=VMEM_LIMIT),
        name="out",
    )(x2, qkvg, qkvg, qkvg, o_a, og_b, w_o_a, w_o_b, w_out, final_g)


def _layer(x, shift_prev, k_past, v_past, wkv_prev, prm, final_g, final_norm):
    batch, t, d = x.shape
    n = batch * t
    x2 = x.reshape(n, d)
    g = prm["norm_g"].reshape(1, d)
    tm = min(n, PROJ_ROWS)
    qkvg = _proj(x2, g, prm["w_in_a"], tm, 3 * D_MODEL)
    pb = _proj(x2, g, prm["w_in_b"], min(n, 512), C_SHIFT).reshape(batch, t, C_SHIFT)
    dim_major = lambda c: None if c is None else jnp.swapaxes(c, -1, -2)
    o_a, k_new, v_new = _sb(qkvg, dim_major(k_past), dim_major(v_past), batch, t)
    og_b, wkv_new = _rwkv(pb, shift_prev, wkv_prev, prm, RWKV_GROUP)
    y = _out(x2, qkvg, o_a, og_b.reshape(n, d), prm["w_o_a"], prm["w_o_b"], prm["w_out"],
             final_g.reshape(1, d), min(n, OUT_ROWS), final_norm)
    return (y.reshape(batch, t, d), jnp.swapaxes(k_new, -1, -2), jnp.swapaxes(v_new, -1, -2),
            pb[:, t - 1:, :], wkv_new)


def kernel(x_prompt, x_sample, cache_sb_k, cache_sb_v, state_shift, state_wkv, norm_g, w_in, mu_shift,
           w0, w2, a0, a2, k_k, k_a, r_k, lnx_g, lnx_b, w_o_a, w_o_b, w_out, final_norm_g):
    depth = w_in.shape[0]
    batch = x_prompt.shape[0]
    xp, xs = x_prompt, x_sample
    outs_p, outs_s = [], []
    for l in range(depth):
        prm = dict(norm_g=norm_g[l],
                   w_in_a=w_in[l][:, :C_QKVG].astype(BF16), w_in_b=w_in[l][:, C_QKVG:].astype(BF16),
                   mu_shift=mu_shift[l], w0=w0[l], w2=w2[l], a0=a0[l], a2=a2[l], k_k=k_k[l], k_a=k_a[l],
                   r_k=r_k[l].reshape(-1), lnx_g=lnx_g[l], lnx_b=lnx_b[l],
                   w_o_a=w_o_a[l].astype(BF16), w_o_b=w_o_b[l].astype(BF16), w_out=w_out[l].astype(BF16))
        last = l == depth - 1
        zero_shift = jnp.zeros((batch, 1, C_SHIFT), F32)
        zero_wkv = jnp.zeros((batch, N_HEADS, D_HEAD, D_HEAD), F32)
        xp, *rest_p = _layer(xp, zero_shift, None, None, zero_wkv, prm, final_norm_g, last)
        xs, *rest_s = _layer(xs, state_shift[l], cache_sb_k[l], cache_sb_v[l], state_wkv[l],
                             prm, final_norm_g, last)
        outs_p.append(rest_p)
        outs_s.append(rest_s)
    stack = lambda outs, i: jnp.stack([o[i] for o in outs])
    return (xp, xs,
            stack(outs_p, 0), stack(outs_p, 1), stack(outs_p, 2), stack(outs_p, 3),
            stack(outs_s, 0), stack(outs_s, 1), stack(outs_s, 2), stack(outs_s, 3))
```
